```python
import jax, jax.numpy as jnp
from jax import lax
import numpy as np

D_MODEL = 4096
BATCH = 4
SEQ = 2048
DEPTH = 1
DEC_BATCH = 128
DEC_SEQ = 1
PAST_LEN = 16384
PAGE_SIZE = 128

N_META = 16
W_A = D_MODEL
N_BLOCKS_A = 16
BLK_A = W_A // N_BLOCKS_A
CONV_A = 4
LRU_C = 8.0
W_B = D_MODEL
CONV_B = 31
N_EXPERTS = 32
TOP_K = 4
D_FF = D_MODEL
SWIGLU_LIMIT = 7.0
SWIGLU_ALPHA = 1.702
MOE_BLOCK = 128
EPS = 1e-6
IN_COLS = 2 * W_A + 2 * W_B + 2 * D_MODEL
SPLITS = [W_A, 2 * W_A, 2 * W_A + W_B, 2 * W_A + 2 * W_B, 2 * W_A + 2 * W_B + D_MODEL]

kernel_name = 'hawk_conformer_moe_decode_step'


def rmsnorm(x, g):
    xf = x.astype(jnp.float32)
    y = xf * lax.rsqrt(jnp.mean(xf * xf, axis=-1, keepdims=True) + EPS)
    return (y * g.astype(jnp.float32)).astype(x.dtype)


def layernorm(x, g, b):
    xf = x.astype(jnp.float32)
    mu = jnp.mean(xf, axis=-1, keepdims=True)
    xc = xf - mu
    y = xc * lax.rsqrt(jnp.mean(xc * xc, axis=-1, keepdims=True) + EPS)
    return (y * g.astype(jnp.float32) + b.astype(jnp.float32)).astype(x.dtype)


def causal_dwconv(hist, x, w, b):
    xx = jnp.concatenate([hist.astype(x.dtype), x], axis=1)
    y = lax.conv_general_dilated(xx, w[:, None, :].astype(x.dtype), window_strides=(1,),
                                 padding='VALID', dimension_numbers=('NWC', 'WIO', 'NWC'),
                                 feature_group_count=x.shape[-1])
    return y + b.astype(x.dtype), xx[:, -(w.shape[0] - 1):]


def block_diag(x, w, b):
    xb = x.reshape(x.shape[:-1] + (N_BLOCKS_A, BLK_A))
    y = jnp.einsum('nthi,hij->nthj', xb, w.astype(jnp.float32)) + b.astype(jnp.float32)
    return y.reshape(x.shape)


def rglru(x, h0, w_r, b_r, w_i, b_i, lam):
    xf = x.astype(jnp.float32)
    r = jax.nn.sigmoid(block_diag(xf, w_r, b_r))
    i = jax.nn.sigmoid(block_diag(xf, w_i, b_i))
    log_a = -LRU_C * r * jax.nn.softplus(-lam.astype(jnp.float32))
    a = jnp.exp(log_a)
    u = jnp.sqrt(-jnp.expm1(2.0 * log_a)) * (i * xf)

    def combine(left, right):
        a1, b1 = left
        a2, b2 = right
        return a1 * a2, a2 * b1 + b2

    a_cum, h = lax.associative_scan(combine, (a, u), axis=1)
    h = h + a_cum * h0[:, None, :].astype(jnp.float32)
    return h.astype(x.dtype), h[:, -1].astype(h0.dtype)


def moe(x2d, w_router, b_router, w_gate, b_gate, w_up, b_up, w_down, b_down):
    n, d = x2d.shape
    p = n * TOP_K
    logits = x2d.astype(jnp.float32) @ w_router.astype(jnp.float32) + b_router.astype(jnp.float32)
    top_v, top_e = lax.top_k(logits, TOP_K)
    gates = jax.nn.softmax(top_v, axis=-1)
    flat_e = top_e.reshape(-1)
    flat_tok = jnp.repeat(jnp.arange(n, dtype=jnp.int32), TOP_K)
    flat_g = gates.reshape(-1)
    order = jnp.argsort(flat_e)
    se = flat_e[order]
    counts = jnp.zeros((N_EXPERTS,), jnp.int32).at[flat_e].add(1)
    pcounts = (counts + MOE_BLOCK - 1) // MOE_BLOCK * MOE_BLOCK
    pend = jnp.cumsum(pcounts)
    pstart = pend - pcounts
    cstart = jnp.cumsum(counts) - counts
    dest = pstart[se] + jnp.arange(p, dtype=jnp.int32) - cstart[se]
    n_blk = -(-p // MOE_BLOCK) + N_EXPERTS
    n_slot = n_blk * MOE_BLOCK
    slot_tok = jnp.zeros((n_slot,), jnp.int32).at[dest].set(flat_tok[order])
    slot_g = jnp.zeros((n_slot,), jnp.float32).at[dest].set(flat_g[order])
    blk_e = jnp.minimum(jnp.searchsorted(pend, jnp.arange(n_blk, dtype=jnp.int32) * MOE_BLOCK, side='right'),
                        N_EXPERTS - 1)
    xs = x2d[slot_tok].reshape(n_blk, MOE_BLOCK, d)

    def expert_block(args):
        xb, e = args
        g = xb @ w_gate[e] + b_gate[e]
        u = xb @ w_up[e] + b_up[e]
        g = jnp.minimum(g, SWIGLU_LIMIT)
        u = jnp.clip(u, -SWIGLU_LIMIT, SWIGLU_LIMIT)
        h = g * jax.nn.sigmoid(SWIGLU_ALPHA * g) * (u + 1.0)
        return h @ w_down[e] + b_down[e]

    ys = lax.map(expert_block, (xs, blk_e)).reshape(n_slot, d)
    out = jax.ops.segment_sum(ys.astype(jnp.float32) * slot_g[:, None], slot_tok, num_segments=n)
    return out.astype(x2d.dtype)


def setup_inputs(seed: int = 0) -> dict:
    key = jax.random.key(seed)
    ks = jax.random.split(key, 32)
    f32 = jnp.float32
    nrm = lambda k, shape, s: jax.random.normal(k, shape, f32) * s
    a0 = jax.random.uniform(ks[10], (DEPTH, W_A), f32, 0.9, 0.999)
    s0 = a0 ** (1.0 / LRU_C)
    lam = jnp.log(s0) - jnp.log1p(-s0)
    return {
        'x_prompt': nrm(ks[0], (BATCH, SEQ, D_MODEL), 1.0),
        'x_sample': nrm(ks[1], (DEC_BATCH, DEC_SEQ, D_MODEL), 1.0),
        'state_lru_h': nrm(ks[2], (DEPTH, DEC_BATCH, W_A), 0.5),
        'state_lru_conv': nrm(ks[3], (DEPTH, DEC_BATCH, CONV_A - 1, W_A), 1.0),
        'state_conf_conv': nrm(ks[4], (DEPTH, DEC_BATCH, CONV_B - 1, W_B), 0.5),
        'meta_tokens': nrm(ks[5], (N_META, D_MODEL), 1.0),
        'norm_mix_g': 1.0 + nrm(ks[6], (DEPTH, D_MODEL), 0.01),
        'w_in': nrm(ks[7], (DEPTH, D_MODEL, IN_COLS), D_MODEL ** -0.5),
        'conv_a_w': nrm(ks[8], (DEPTH, CONV_A, W_A), CONV_A ** -0.5),
        'conv_a_b': nrm(ks[9], (DEPTH, W_A), 0.01),
        'w_r': nrm(ks[11], (DEPTH, N_BLOCKS_A, BLK_A, BLK_A), BLK_A ** -0.5),
        'b_r': nrm(ks[12], (DEPTH, N_BLOCKS_A, BLK_A), 0.01),
        'w_i': nrm(ks[13], (DEPTH, N_BLOCKS_A, BLK_A, BLK_A), BLK_A ** -0.5),
        'b_i': nrm(ks[14], (DEPTH, N_BLOCKS_A, BLK_A), 0.01),
        'lru_lambda': lam,
        'conv_b_w': nrm(ks[15], (DEPTH, CONV_B, W_B), CONV_B ** -0.5),
        'conv_b_b': nrm(ks[16], (DEPTH, W_B), 0.01),
        'ln_b_g': 1.0 + nrm(ks[17], (DEPTH, W_B), 0.01),
        'ln_b_b': nrm(ks[18], (DEPTH, W_B), 0.01),
        'w_a_out': nrm(ks[19], (DEPTH, W_A, D_MODEL), W_A ** -0.5),
        'w_b_out': nrm(ks[20], (DEPTH, W_B, D_MODEL), W_B ** -0.5),
        'w_o': nrm(ks[21], (DEPTH, D_MODEL, D_MODEL), D_MODEL ** -0.5),
        'norm_ffn_g': 1.0 + nrm(ks[22], (DEPTH, D_MODEL), 0.01),
        'w_router': nrm(ks[23], (DEPTH, D_MODEL, N_EXPERTS), D_MODEL ** -0.5),
        'b_router': nrm(ks[24], (DEPTH, N_EXPERTS), 0.01),
        'w_gate': nrm(ks[25], (DEPTH, N_EXPERTS, D_MODEL, D_FF), D_MODEL ** -0.5),
        'b_gate': nrm(ks[26], (DEPTH, N_EXPERTS, D_FF), 0.01),
        'w_up': nrm(ks[27], (DEPTH, N_EXPERTS, D_MODEL, D_FF), D_MODEL ** -0.5),
        'b_up': nrm(ks[28], (DEPTH, N_EXPERTS, D_FF), 0.01),
        'w_down': nrm(ks[29], (DEPTH, N_EXPERTS, D_FF, D_MODEL), D_FF ** -0.5),
        'b_down': nrm(ks[30], (DEPTH, N_EXPERTS, D_MODEL), 0.01),
        'norm_final_g': 1.0 + nrm(ks[31], (D_MODEL,), 0.01),
    }


def reference(x_prompt, x_sample, state_lru_h, state_lru_conv, state_conf_conv, meta_tokens,
              norm_mix_g, w_in, conv_a_w, conv_a_b, w_r, b_r, w_i, b_i, lru_lambda,
              conv_b_w, conv_b_b, ln_b_g, ln_b_b, w_a_out, w_b_out, w_o, norm_ffn_g,
              w_router, b_router, w_gate, b_gate, w_up, b_up, w_down, b_down, norm_final_g):

    def run_layer(x, hist_a, h0, hist_b, l):
        n, t, d = x.shape
        xn = rmsnorm(x, norm_mix_g[l])
        proj = xn @ w_in[l]
        xa, ga, x_glu, x_gate, gate_a, gate_b = jnp.split(proj, SPLITS, axis=-1)
        ca, new_hist_a = causal_dwconv(hist_a, xa, conv_a_w[l], conv_a_b[l])
        ha, new_h = rglru(ca, h0, w_r[l], b_r[l], w_i[l], b_i[l], lru_lambda[l])
        ya = ha * jax.nn.gelu(ga, approximate=True)
        glu = x_glu * jax.nn.sigmoid(x_gate)
        cb, new_hist_b = causal_dwconv(hist_b, glu, conv_b_w[l], conv_b_b[l])
        yb = jax.nn.silu(layernorm(cb, ln_b_g[l], ln_b_b[l]))
        merged = jax.nn.sigmoid(gate_a) * (ya @ w_a_out[l]) + jax.nn.sigmoid(gate_b) * (yb @ w_b_out[l])
        x = x + merged @ w_o[l]
        xn2 = rmsnorm(x, norm_ffn_g[l]).reshape(n * t, d)
        x = x + moe(xn2, w_router[l], b_router[l], w_gate[l], b_gate[l], w_up[l], b_up[l],
                    w_down[l], b_down[l]).reshape(n, t, d)
        return x, new_hist_a, new_h, new_hist_b

    xp = jnp.concatenate([jnp.broadcast_to(meta_tokens.astype(x_prompt.dtype)[None], (BATCH, N_META, D_MODEL)),
                          x_prompt], axis=1)
    xs = x_sample
    p_ha, p_h, p_hb, s_ha, s_h, s_hb = [], [], [], [], [], []
    for l in range(DEPTH):
        xp, ha, h, hb = run_layer(xp,
                                  jnp.zeros((BATCH, CONV_A - 1, W_A), state_lru_conv.dtype),
                                  jnp.zeros((BATCH, W_A), state_lru_h.dtype),
                                  jnp.zeros((BATCH, CONV_B - 1, W_B), state_conf_conv.dtype), l)
        p_ha.append(ha); p_h.append(h); p_hb.append(hb)
        xs, ha, h, hb = run_layer(xs, state_lru_conv[l], state_lru_h[l], state_conf_conv[l], l)
        s_ha.append(ha); s_h.append(h); s_hb.append(hb)

    y_prompt = rmsnorm(xp, norm_final_g)[:, N_META:]
    y_sample = rmsnorm(xs, norm_final_g)
    return (y_prompt, y_sample,
            jnp.stack(p_h), jnp.stack(p_ha), jnp.stack(p_hb),
            jnp.stack(s_h), jnp.stack(s_ha), jnp.stack(s_hb))
```

```python
import functools
import math

import jax
import jax.numpy as jnp
from jax import lax
from jax.experimental import pallas as pl
from jax.experimental.pallas import tpu as pltpu

_F32 = jnp.float32
_BF16 = jnp.bfloat16

_EPS = 1e-6
_LRU_C = 8.0
_SWIGLU_LIMIT = 7.0
_SWIGLU_ALPHA = 1.702
_TOP_K = 4
_HEAD = 256

_V7X_VMEM_BYTES = 64 * 1024 * 1024
_VMEM_CAP = _V7X_VMEM_BYTES - 6 * 1024 * 1024
_SUBLANES_BF16 = 16


def _vmem_limit(estimate_bytes):
    return int(min(max(estimate_bytes + (8 << 20), 24 << 20), _VMEM_CAP))


def _largest_divisor(n, cap, mult):
    best = None
    for d in range(mult, min(n, cap) + 1, mult):
        if n % d == 0:
            best = d
    assert best is not None, (n, cap, mult)
    return best


def _round_up(n, m):
    return (n + m - 1) // m * m


def _sigmoid(x):
    return jax.nn.sigmoid(x)


def _gelu_tanh(x):
    c = math.sqrt(2.0 / math.pi)
    return 0.5 * x * (1.0 + jnp.tanh(c * (x + 0.044715 * (x * x * x))))


def _expm1(x):
    u = jnp.exp(x)
    um1 = u - 1.0
    return jnp.where(um1 == 0.0, x, jnp.where(um1 == -1.0, -1.0, um1 * (x / jnp.log(u))))


def _softplus(x):
    return jnp.maximum(x, 0.0) + jnp.log1p(jnp.exp(-jnp.abs(x)))


def _rmsnorm_kernel(x_ref, g_ref, o_ref):
    x = x_ref[...]
    ms = jnp.mean(x * x, axis=-1, keepdims=True)
    o_ref[...] = ((x * lax.rsqrt(ms + _EPS)) * g_ref[...]).astype(o_ref.dtype)


def _rmsnorm_rows(x, g, out_dtype):
    rows, d = x.shape
    tr = _largest_divisor(rows, 256, _SUBLANES_BF16)
    return pl.pallas_call(
        _rmsnorm_kernel,
        out_shape=jax.ShapeDtypeStruct((rows, d), out_dtype),
        grid=(rows // tr,),
        in_specs=[pl.BlockSpec((tr, d), lambda i: (i, 0)),
                  pl.BlockSpec((1, d), lambda i: (0, 0))],
        out_specs=pl.BlockSpec((tr, d), lambda i: (i, 0)),
        compiler_params=pltpu.CompilerParams(
            dimension_semantics=("arbitrary",),
            vmem_limit_bytes=_vmem_limit(4 * tr * d * 4)),
        name="rmsnorm_rows",
    )(x, g.reshape(1, d))


def _gmm_kernel(blk_ref, eid_ref, nsub_ref, *refs, n_lhs, n_w, has_bias, n_extra,
                pairs, sub, cast_rows, epilogue):
    del blk_ref, eid_ref
    pos = 0
    lhs_refs = refs[pos:pos + n_lhs]; pos += n_lhs
    w_refs = refs[pos:pos + n_w]; pos += n_w
    if has_bias:
        b_refs = refs[pos:pos + n_w]; pos += n_w
    ex_refs = refs[pos:pos + n_extra]; pos += n_extra
    o_ref = refs[pos]; pos += 1
    wbf_refs = refs[pos:pos + n_w]

    nsub = nsub_ref[pl.program_id(0)]

    @pl.when(nsub > 0)
    def _():
        k_dim = w_refs[0].shape[0]

        def cast_body(c, carry):
            r = pl.multiple_of(c * cast_rows, cast_rows)
            for w_ref, wbf in zip(w_refs, wbf_refs):
                wbf[pl.ds(r, cast_rows), :] = w_ref[pl.ds(r, cast_rows), :].astype(_BF16)
            return carry

        lax.fori_loop(0, k_dim // cast_rows, cast_body, 0)

        def sub_body(s, carry):
            r = pl.multiple_of(s * sub, sub)
            prods = []
            for (li, wi) in pairs:
                p = jnp.dot(lhs_refs[li][pl.ds(r, sub), :], wbf_refs[wi][...],
                            preferred_element_type=_F32)
                if has_bias:
                    p = p + b_refs[wi][...]
                prods.append(p)
            extras = [e[pl.ds(r, sub), :] for e in ex_refs]
            o_ref[pl.ds(r, sub), :] = epilogue(prods, extras).astype(o_ref.dtype)
            return carry

        lax.fori_loop(0, nsub, sub_body, 0)


def _gmm(lhs, ws, pairs, tables, *, tm, tn, sub, out_dtype, epilogue, name,
         biases=None, extras=()):
    blk, eid, nsub = tables
    n_chunks = blk.shape[0]
    rows, k_dim = lhs[0].shape
    n_out = ws[0].shape[2]
    nj = n_out // tn
    assert n_out % tn == 0 and tm % sub == 0 and rows % tm == 0
    cast_rows = _largest_divisor(k_dim, 512, _SUBLANES_BF16)

    def jj(v, j, nsub_ref):
        return jnp.where(nsub_ref[v] > 0, j, nj - 1)

    in_specs = []
    for _ in lhs:
        in_specs.append(pl.BlockSpec((tm, k_dim), lambda v, j, b, e, n: (b[v], 0)))
    for _ in ws:
        in_specs.append(pl.BlockSpec((None, k_dim, tn),
                                     lambda v, j, b, e, n: (e[v], 0, jj(v, j, n))))
    has_bias = biases is not None
    if has_bias:
        for _ in ws:
            in_specs.append(pl.BlockSpec((None, 1, tn),
                                         lambda v, j, b, e, n: (e[v], 0, jj(v, j, n))))
    for (_, off) in extras:
        in_specs.append(pl.BlockSpec(
            (tm, tn), lambda v, j, b, e, n, off=off: (b[v], off + jj(v, j, n))))
    out_spec = pl.BlockSpec((tm, tn), lambda v, j, b, e, n: (b[v], jj(v, j, n)))

    out_bytes = jnp.dtype(out_dtype).itemsize
    est = (2 * len(lhs) * tm * k_dim * 2 + len(ws) * (2 * k_dim * tn * 4 + k_dim * tn * 2)
           + 2 * len(extras) * tm * tn * 4 + 2 * tm * tn * out_bytes
           + (2 + len(pairs)) * sub * tn * 4)
    kernel = functools.partial(
        _gmm_kernel, n_lhs=len(lhs), n_w=len(ws), has_bias=has_bias, n_extra=len(extras),
        pairs=tuple(pairs), sub=sub, cast_rows=cast_rows, epilogue=epilogue)
    args = list(lhs) + list(ws) + (list(biases) if has_bias else []) + [a for (a, _) in extras]
    return pl.pallas_call(
        kernel,
        out_shape=jax.ShapeDtypeStruct((rows, n_out), out_dtype),
        grid_spec=pltpu.PrefetchScalarGridSpec(
            num_scalar_prefetch=3,
            grid=(n_chunks, nj),
            in_specs=in_specs,
            out_specs=out_spec,
            scratch_shapes=[pltpu.VMEM((k_dim, tn), _BF16) for _ in ws]),
        compiler_params=pltpu.CompilerParams(
            dimension_semantics=("arbitrary", "arbitrary"),
            vmem_limit_bytes=_vmem_limit(est)),
        name=name,
    )(blk, eid, nsub, *args)


def _dense_tables(rows, tm, sub):
    n = rows // tm
    return (jnp.arange(n, dtype=jnp.int32), jnp.zeros((n,), jnp.int32),
            jnp.full((n,), tm // sub, jnp.int32))


def _lru_gates(ca, wr_ref, br_ref, wi_ref, bi_ref, lam_ref):
    cab = ca.astype(_BF16)
    n_heads = wr_ref.shape[0]
    zr, zi = [], []
    for hh in range(n_heads):
        c_h = cab[:, hh * _HEAD:(hh + 1) * _HEAD]
        zr.append(jnp.dot(c_h, wr_ref[hh].astype(_BF16), preferred_element_type=_F32))
        zi.append(jnp.dot(c_h, wi_ref[hh].astype(_BF16), preferred_element_type=_F32))
    zr = jnp.concatenate(zr, axis=-1) if n_heads > 1 else zr[0]
    zi = jnp.concatenate(zi, axis=-1) if n_heads > 1 else zi[0]
    r = _sigmoid(zr + br_ref[...])
    i = _sigmoid(zi + bi_ref[...])
    log_a = (-_LRU_C * r) * _softplus(-lam_ref[...])
    a = jnp.exp(log_a)
    u = jnp.sqrt(-_expm1(2.0 * log_a)) * (i * ca)
    return a, u


def _branch_a_kernel(xa_ref, ga_ref, cw_ref, cb_ref, wr_ref, br_ref, wi_ref, bi_ref, lam_ref,
                     ya_ref, h_ref, hist_ref, buf, hcar, *, tt, kw):
    t = pl.program_id(2)
    nt = pl.num_programs(2)
    hb = 8

    @pl.when(t == 0)
    def _():
        buf[0:hb, :] = jnp.zeros((hb, buf.shape[1]), _F32)
        hcar[...] = jnp.zeros_like(hcar)

    buf[hb:hb + tt, :] = xa_ref[...]
    ca = cb_ref[...] + cw_ref[0:1, :] * buf[hb - (kw - 1):hb - (kw - 1) + tt, :]
    for k in range(1, kw):
        s = hb - (kw - 1) + k
        ca = ca + cw_ref[k:k + 1, :] * buf[s:s + tt, :]
    tail = buf[hb + tt - (kw - 1):hb + tt, :]
    buf[hb - (kw - 1):hb, :] = tail

    a, u = _lru_gates(ca, wr_ref, br_ref, wi_ref, bi_ref, lam_ref)

    row = lax.broadcasted_iota(jnp.int32, a.shape, 0)
    d = 1
    while d < tt:
        keep = row >= d
        a_sh = pltpu.roll(a, d, 0)
        u_sh = pltpu.roll(u, d, 0)
        u = jnp.where(keep, a * u_sh + u, u)
        a = jnp.where(keep, a * a_sh, a)
        d *= 2
    h = u + a * hcar[...]
    h_last = h[tt - 1:tt, :]
    hcar[...] = h_last
    ya_ref[...] = (h * _gelu_tanh(ga_ref[...])).astype(ya_ref.dtype)

    @pl.when(t == nt - 1)
    def _():
        h_ref[...] = h_last
        hist_ref[...] = tail


def _branch_a_prompt(proj, p, *, n_seq, t_len, d, rows_total):
    kw = p["conv_a_w"].shape[0]
    tt = _largest_divisor(t_len, 688, _SUBLANES_BF16)
    cw = _largest_divisor(d, 512, _HEAD)
    nt, nc = t_len // tt, d // cw
    hpc = cw // _HEAD
    ga_off = d // cw
    vec = lambda b, c, t: (0, c)
    kernel = functools.partial(_branch_a_kernel, tt=tt, kw=kw)
    return pl.pallas_call(
        kernel,
        out_shape=(jax.ShapeDtypeStruct((rows_total, d), _BF16),
                   jax.ShapeDtypeStruct((n_seq, 1, d), _F32),
                   jax.ShapeDtypeStruct((n_seq, kw - 1, d), _F32)),
        grid=(n_seq, nc, nt),
        in_specs=[
            pl.BlockSpec((tt, cw), lambda b, c, t: (b * nt + t, c)),
            pl.BlockSpec((tt, cw), lambda b, c, t: (b * nt + t, ga_off + c)),
            pl.BlockSpec((kw, cw), vec),
            pl.BlockSpec((1, cw), vec),
            pl.BlockSpec((hpc, _HEAD, _HEAD), lambda b, c, t: (c, 0, 0)),
            pl.BlockSpec((1, cw), vec),
            pl.BlockSpec((hpc, _HEAD, _HEAD), lambda b, c, t: (c, 0, 0)),
            pl.BlockSpec((1, cw), vec),
            pl.BlockSpec((1, cw), vec),
        ],
        out_specs=(
            pl.BlockSpec((tt, cw), lambda b, c, t: (b * nt + t, c)),
            pl.BlockSpec((None, 1, cw), lambda b, c, t: (b, 0, c)),
            pl.BlockSpec((None, kw - 1, cw), lambda b, c, t: (b, 0, c)),
        ),
        scratch_shapes=[pltpu.VMEM((8 + tt, cw), _F32), pltpu.VMEM((1, cw), _F32)],
        compiler_params=pltpu.CompilerParams(
            dimension_semantics=("arbitrary", "arbitrary", "arbitrary"),
            vmem_limit_bytes=_vmem_limit(24 * tt * cw * 4)),
        name="branch_a_prompt",
    )(proj, proj, p["conv_a_w"], p["conv_a_b"], p["w_r"], p["b_r"], p["w_i"], p["b_i"],
      p["lru_lambda"])


def _layernorm_silu(cb, g, b):
    mu = jnp.mean(cb, axis=-1, keepdims=True)
    xc = cb - mu
    y = xc * lax.rsqrt(jnp.mean(xc * xc, axis=-1, keepdims=True) + _EPS)
    y = y * g + b
    return y * _sigmoid(y)


def _branch_b_kernel(xg_ref, gt_ref, cw_ref, cb_ref, lg_ref, lb_ref, yb_ref, hist_ref,
                     buf, cbuf, *, tt, kw, lane_chunk):
    t = pl.program_id(1)
    nt = pl.num_programs(1)
    hb = _round_up(kw - 1, 8)
    d = buf.shape[1]

    @pl.when(t == 0)
    def _():
        buf[0:hb, :] = jnp.zeros((hb, d), _F32)

    buf[hb:hb + tt, :] = xg_ref[...] * _sigmoid(gt_ref[...])
    first = hb - (kw - 1)
    for c0 in range(0, d, lane_chunk):
        cs = slice(c0, c0 + lane_chunk)
        acc = cb_ref[:, cs] + cw_ref[0:1, cs] * buf[first:first + tt, cs]
        for k in range(1, kw):
            acc = acc + cw_ref[k:k + 1, cs] * buf[first + k:first + k + tt, cs]
        cbuf[:, cs] = acc
    tail = buf[hb + tt - (kw - 1):hb + tt, :]
    buf[first:hb, :] = tail
    yb_ref[...] = _layernorm_silu(cbuf[...], lg_ref[...], lb_ref[...]).astype(yb_ref.dtype)

    @pl.when(t == nt - 1)
    def _():
        hist_ref[...] = tail


def _branch_b_prompt(proj, p, *, n_seq, t_len, d, rows_total):
    kw = p["conv_b_w"].shape[0]
    tt = _largest_divisor(t_len, 48, _SUBLANES_BF16)
    nt = t_len // tt
    hb = _round_up(kw - 1, 8)
    vec = lambda b, t: (0, 0)
    kernel = functools.partial(_branch_b_kernel, tt=tt, kw=kw,
                               lane_chunk=_largest_divisor(d, 512, 128))
    return pl.pallas_call(
        kernel,
        out_shape=(jax.ShapeDtypeStruct((rows_total, d), _BF16),
                   jax.ShapeDtypeStruct((n_seq, kw - 1, d), _F32)),
        grid=(n_seq, nt),
        in_specs=[
            pl.BlockSpec((tt, d), lambda b, t: (b * nt + t, 2)),
            pl.BlockSpec((tt, d), lambda b, t: (b * nt + t, 3)),
            pl.BlockSpec((kw, d), vec),
            pl.BlockSpec((1, d), vec),
            pl.BlockSpec((1, d), vec),
            pl.BlockSpec((1, d), vec),
        ],
        out_specs=(
            pl.BlockSpec((tt, d), lambda b, t: (b * nt + t, 0)),
            pl.BlockSpec((None, kw - 1, d), lambda b, t: (b, 0, 0)),
        ),
        scratch_shapes=[pltpu.VMEM((hb + tt, d), _F32), pltpu.VMEM((tt, d), _F32)],
        compiler_params=pltpu.CompilerParams(
            dimension_semantics=("arbitrary", "arbitrary"),
            vmem_limit_bytes=_vmem_limit(16 * (hb + tt) * d * 4)),
        name="branch_b_prompt",
    )(proj, proj, p["conv_b_w"], p["conv_b_b"], p["ln_b_g"], p["ln_b_b"])


def _sample_kernel(xa_ref, ga_ref, xg_ref, gt_ref, sh_ref, sa_ref, sb_ref,
                   caw_ref, cab_ref, wr_ref, br_ref, wi_ref, bi_ref, lam_ref,
                   cbw_ref, cbb_ref, lg_ref, lb_ref, ya_in, yb_in,
                   ya_ref, yb_ref, oh_ref, oa_ref, ob_ref, glu_s, ca_s, cb_s,
                   *, tr, n_real, kwa, kwb):
    del ya_in, yb_in
    i = pl.program_id(0)

    @pl.when(i < n_real)
    def _():
        glu_s[...] = xg_ref[...] * _sigmoid(gt_ref[...])

        def row_body(n, carry):
            hb = sb_ref[n]
            g_row = glu_s[pl.ds(n, 1), :]
            cb_s[pl.ds(n, 1), :] = (
                jnp.sum(hb * cbw_ref[0:kwb - 1, :], axis=0, keepdims=True)
                + cbw_ref[kwb - 1:kwb, :] * g_row + cbb_ref[...])
            ob_ref[n, 0:kwb - 2, :] = hb[1:kwb - 1, :]
            ob_ref[n, kwb - 2:kwb - 1, :] = g_row
            ha = sa_ref[n]
            x_row = xa_ref[pl.ds(n, 1), :]
            ca_s[pl.ds(n, 1), :] = (
                jnp.sum(ha * caw_ref[0:kwa - 1, :], axis=0, keepdims=True)
                + caw_ref[kwa - 1:kwa, :] * x_row + cab_ref[...])
            oa_ref[n, 0:kwa - 2, :] = ha[1:kwa - 1, :]
            oa_ref[n, kwa - 2:kwa - 1, :] = x_row
            return carry

        lax.fori_loop(0, tr, row_body, 0)

        ca = ca_s[...]
        a, u = _lru_gates(ca, wr_ref, br_ref, wi_ref, bi_ref, lam_ref)
        h = a * sh_ref[...] + u
        oh_ref[...] = h
        ya_ref[...] = (h * _gelu_tanh(ga_ref[...])).astype(ya_ref.dtype)
        yb_ref[...] = _layernorm_silu(cb_s[...], lg_ref[...], lb_ref[...]).astype(yb_ref.dtype)

    @pl.when(i >= n_real)
    def _():
        ya_ref[...] = jnp.zeros_like(ya_ref)
        yb_ref[...] = jnp.zeros_like(yb_ref)


def _branches_sample(proj, ya, yb, state_h, state_a, state_b, p, *, row0, n_s, d, rows_total):
    tr = _SUBLANES_BF16
    kwa, kwb = p["conv_a_w"].shape[0], p["conv_b_w"].shape[0]
    n_real = n_s // tr
    n_blocks = (rows_total - row0) // tr
    b0 = row0 // tr
    n_heads = d // _HEAD
    full = lambda i: (0, 0)
    st = lambda i: (jnp.minimum(i, n_real - 1), 0)
    st3 = lambda i: (jnp.minimum(i, n_real - 1), 0, 0)
    kernel = functools.partial(_sample_kernel, tr=tr, n_real=n_real, kwa=kwa, kwb=kwb)
    col = lambda c: pl.BlockSpec((tr, d), lambda i, c=c: (b0 + i, c))
    est = 4 * tr * (_round_up(kwb - 1, 8) + 8) * d * 4 + 16 * tr * d * 4 + (kwb + 16) * d * 4
    return pl.pallas_call(
        kernel,
        out_shape=(jax.ShapeDtypeStruct((rows_total, d), _BF16),
                   jax.ShapeDtypeStruct((rows_total, d), _BF16),
                   jax.ShapeDtypeStruct((n_s, d), _F32),
                   jax.ShapeDtypeStruct((n_s, kwa - 1, d), _F32),
                   jax.ShapeDtypeStruct((n_s, kwb - 1, d), _F32)),
        grid=(n_blocks,),
        in_specs=[
            col(0), col(1), col(2), col(3),
            pl.BlockSpec((tr, d), st),
            pl.BlockSpec((tr, kwa - 1, d), st3),
            pl.BlockSpec((tr, kwb - 1, d), st3),
            pl.BlockSpec((kwa, d), full), pl.BlockSpec((1, d), full),
            pl.BlockSpec((n_heads, _HEAD, _HEAD), lambda i: (0, 0, 0)), pl.BlockSpec((1, d), full),
            pl.BlockSpec((n_heads, _HEAD, _HEAD), lambda i: (0, 0, 0)), pl.BlockSpec((1, d), full),
            pl.BlockSpec((1, d), full),
            pl.BlockSpec((kwb, d), full), pl.BlockSpec((1, d), full),
            pl.BlockSpec((1, d), full), pl.BlockSpec((1, d), full),
            pl.BlockSpec(memory_space=pl.ANY), pl.BlockSpec(memory_space=pl.ANY),
        ],
        out_specs=(
            pl.BlockSpec((tr, d), lambda i: (b0 + i, 0)),
            pl.BlockSpec((tr, d), lambda i: (b0 + i, 0)),
            pl.BlockSpec((tr, d), st),
            pl.BlockSpec((tr, kwa - 1, d), st3),
            pl.BlockSpec((tr, kwb - 1, d), st3),
        ),
        scratch_shapes=[pltpu.VMEM((tr, d), _F32)] * 3,
        input_output_aliases={18: 0, 19: 1},
        compiler_params=pltpu.CompilerParams(
            dimension_semantics=("arbitrary",),
            vmem_limit_bytes=_vmem_limit(est)),
        name="branches_sample",
    )(proj, proj, proj, proj, state_h, state_a, state_b,
      p["conv_a_w"], p["conv_a_b"], p["w_r"], p["b_r"], p["w_i"], p["b_i"], p["lru_lambda"],
      p["conv_b_w"], p["conv_b_b"], p["ln_b_g"], p["ln_b_b"], ya, yb)


def _router_kernel(x_ref, g_ref, wr_ref, br_ref, xn_ref, te_ref, tg_ref, rk_ref, cnt_ref,
                   carry, *, n_exp):
    i = pl.program_id(0)

    @pl.when(i == 0)
    def _():
        carry[...] = jnp.zeros_like(carry)

    x = x_ref[...]
    ms = jnp.mean(x * x, axis=-1, keepdims=True)
    xn = (x * lax.rsqrt(ms + _EPS)) * g_ref[...]
    xn_ref[...] = xn

    w = wr_ref[...]
    xh = xn.astype(_BF16)
    xl = (xn - xh.astype(_F32)).astype(_BF16)
    wh = w.astype(_BF16)
    wl = (w - wh.astype(_F32)).astype(_BF16)
    logits = (jnp.dot(xh, wh, preferred_element_type=_F32)
              + (jnp.dot(xl, wh, preferred_element_type=_F32)
                 + jnp.dot(xh, wl, preferred_element_type=_F32))) + br_ref[...]

    tb = x.shape[0]
    lane = lax.broadcasted_iota(jnp.int32, (tb, n_exp), 1)
    work = logits
    vals, idxs, hots = [], [], []
    for _ in range(_TOP_K):
        m = jnp.max(work, axis=-1, keepdims=True)
        idx = jnp.min(jnp.where(work == m, lane, n_exp), axis=-1, keepdims=True)
        hot = lane == idx
        vals.append(m); idxs.append(idx); hots.append(hot)
        work = jnp.where(hot, -jnp.inf, work)
    es = [jnp.exp(v - vals[0]) for v in vals]
    den = es[0] + es[1] + es[2] + es[3]
    tg_ref[...] = jnp.concatenate([e / den for e in es], axis=-1)
    te_ref[...] = jnp.concatenate(idxs, axis=-1)

    sel = (hots[0] | hots[1] | hots[2] | hots[3]).astype(_F32)
    r_i = lax.broadcasted_iota(jnp.int32, (tb, tb), 0)
    c_i = lax.broadcasted_iota(jnp.int32, (tb, tb), 1)
    tri = (c_i < r_i).astype(_BF16)
    before = jnp.dot(tri, sel.astype(_BF16), preferred_element_type=_F32) + carry[...]
    ranks = [jnp.sum(jnp.where(h, before, 0.0), axis=-1, keepdims=True) for h in hots]
    rk_ref[...] = jnp.concatenate(ranks, axis=-1).astype(jnp.int32)
    total = carry[...] + jnp.sum(sel, axis=0, keepdims=True)
    carry[...] = total
    cnt_ref[...] = total.astype(jnp.int32)


def _router(x2, g, w_router, b_router):
    rows, d = x2.shape
    n_exp = w_router.shape[1]
    tb = _largest_divisor(rows, 256, 8)
    tok = lambda i: (i, 0)
    full = lambda i: (0, 0)
    return pl.pallas_call(
        functools.partial(_router_kernel, n_exp=n_exp),
        out_shape=(jax.ShapeDtypeStruct((rows, d), _F32),
                   jax.ShapeDtypeStruct((rows, _TOP_K), jnp.int32),
                   jax.ShapeDtypeStruct((rows, _TOP_K), _F32),
                   jax.ShapeDtypeStruct((rows, _TOP_K), jnp.int32),
                   jax.ShapeDtypeStruct((1, n_exp), jnp.int32)),
        grid=(rows // tb,),
        in_specs=[pl.BlockSpec((tb, d), tok), pl.BlockSpec((1, d), full),
                  pl.BlockSpec((d, n_exp), full), pl.BlockSpec((1, n_exp), full)],
        out_specs=(pl.BlockSpec((tb, d), tok), pl.BlockSpec((tb, _TOP_K), tok),
                   pl.BlockSpec((tb, _TOP_K), tok), pl.BlockSpec((tb, _TOP_K), tok),
                   pl.BlockSpec((1, n_exp), full)),
        scratch_shapes=[pltpu.VMEM((1, n_exp), _F32)],
        compiler_params=pltpu.CompilerParams(
            dimension_semantics=("arbitrary",),
            vmem_limit_bytes=_vmem_limit(8 * tb * d * 4)),
        name="router",
    )(x2, g.reshape(1, d), w_router, b_router.reshape(1, n_exp))


def _row_copy(src_hbm, src_row, dst_vmem, dst_row, sem):
    return pltpu.make_async_copy(src_hbm.at[pl.ds(src_row, 1), :],
                                 dst_vmem.at[pl.ds(dst_row, 1), :], sem)


def _gather_kernel(tok_ref, first_ref, rows_ref, last_ref, blk_ref, x_hbm, o_ref, buf, sem,
                   *, gb):
    del last_ref, blk_ref
    v = pl.program_id(0)
    s = pl.program_id(1)
    n_valid = rows_ref[v] - s * gb

    @pl.when(n_valid > 0)
    def _():
        base = first_ref[v] + s * gb

        def token(r):
            return tok_ref[jnp.where(r < n_valid, base + r, 0)]

        def issue(r, carry):
            _row_copy(x_hbm, token(r), buf, r, sem).start()
            return carry

        lax.fori_loop(0, gb, issue, 0)

        def drain(r, carry):
            _row_copy(x_hbm, 0, buf, r, sem).wait()
            return carry

        lax.fori_loop(0, gb, drain, 0)
        o_ref[...] = buf[...].astype(o_ref.dtype)


def _dispatch_gather(xn, sorted_tok, chunk_first, chunk_rows, chunk_last, chunk_blk, *, tm, gb):
    n_chunks = chunk_rows.shape[0]
    d = xn.shape[1]
    spc = tm // gb

    def out_map(v, s, tok, first, rows, last, blk):
        s_eff = jnp.where(rows[v] > 0, jnp.minimum(s, last[v]), last[v])
        return (blk[v] * spc + s_eff, 0)

    return pl.pallas_call(
        functools.partial(_gather_kernel, gb=gb),
        out_shape=jax.ShapeDtypeStruct((n_chunks * tm, d), _BF16),
        grid_spec=pltpu.PrefetchScalarGridSpec(
            num_scalar_prefetch=5,
            grid=(n_chunks, spc),
            in_specs=[pl.BlockSpec(memory_space=pl.ANY)],
            out_specs=pl.BlockSpec((gb, d), out_map),
            scratch_shapes=[pltpu.VMEM((gb, d), _F32), pltpu.SemaphoreType.DMA(())]),
        compiler_params=pltpu.CompilerParams(
            dimension_semantics=("arbitrary", "arbitrary"),
            vmem_limit_bytes=_vmem_limit(4 * gb * d * 4)),
        name="dispatch_gather",
    )(sorted_tok, chunk_first, chunk_rows, chunk_last, chunk_blk, xn)


def _combine_kernel(slot_ref, ys_hbm, g_ref, x_ref, gf_ref, o_ref, buf, sem, *, tb):
    base = pl.program_id(0) * (tb * _TOP_K)

    def issue(r, carry):
        for k in range(_TOP_K):
            _row_copy(ys_hbm, slot_ref[base + r * _TOP_K + k], buf.at[k], r, sem).start()
        return carry

    lax.fori_loop(0, tb, issue, 0)

    def drain(r, carry):
        for k in range(_TOP_K):
            _row_copy(ys_hbm, 0, buf.at[k], r, sem).wait()
        return carry

    lax.fori_loop(0, tb, drain, 0)

    g = g_ref[...]
    moe = g[:, 0:1] * buf[0]
    for k in range(1, _TOP_K):
        moe = moe + g[:, k:k + 1] * buf[k]
    x = x_ref[...] + moe
    ms = jnp.mean(x * x, axis=-1, keepdims=True)
    o_ref[...] = (x * lax.rsqrt(ms + _EPS)) * gf_ref[...]


def _combine(ys, slots, gates, x2, g_final):
    rows, d = x2.shape
    tb = _largest_divisor(rows, 64, 8)
    tok = lambda i, s: (i, 0)
    return pl.pallas_call(
        functools.partial(_combine_kernel, tb=tb),
        out_shape=jax.ShapeDtypeStruct((rows, d), _F32),
        grid_spec=pltpu.PrefetchScalarGridSpec(
            num_scalar_prefetch=1,
            grid=(rows // tb,),
            in_specs=[pl.BlockSpec(memory_space=pl.ANY),
                      pl.BlockSpec((tb, _TOP_K), tok),
                      pl.BlockSpec((tb, d), tok),
                      pl.BlockSpec((1, d), lambda i, s: (0, 0))],
            out_specs=pl.BlockSpec((tb, d), tok),
            scratch_shapes=[pltpu.VMEM((_TOP_K, tb, d), _F32), pltpu.SemaphoreType.DMA(())]),
        compiler_params=pltpu.CompilerParams(
            dimension_semantics=("arbitrary",),
            vmem_limit_bytes=_vmem_limit((_TOP_K + 8) * tb * d * 4)),
        name="combine",
    )(slots, ys, gates, x2, g_final.reshape(1, d))


def _moe_tables(counts, top_e, rank, *, tm, sub, n_chunks):
    n_exp = counts.shape[0]
    chunks_e = (counts + tm - 1) // tm
    chunk_end = jnp.cumsum(chunks_e)
    chunk_start = chunk_end - chunks_e
    cstart = jnp.cumsum(counts) - counts
    n_used = chunk_end[-1]
    v = jnp.arange(n_chunks, dtype=jnp.int32)
    v_eff = jnp.minimum(v, n_used - 1)
    e_v = jnp.minimum(jnp.searchsorted(chunk_end, v_eff, side="right"), n_exp - 1).astype(jnp.int32)
    within = v_eff - chunk_start[e_v]
    rows_eff = jnp.clip(counts[e_v] - within * tm, 0, tm)
    rows_v = jnp.where(v < n_used, rows_eff, 0).astype(jnp.int32)
    nsub_v = (rows_v + sub - 1) // sub
    last_v = ((rows_eff + sub - 1) // sub - 1).astype(jnp.int32)
    first_v = (cstart[e_v] + within * tm).astype(jnp.int32)
    slot = chunk_start[top_e] * tm + rank
    sorted_pos = cstart[top_e] + rank
    tables = (v_eff.astype(jnp.int32), e_v, nsub_v.astype(jnp.int32))
    return tables, rows_v, last_v, first_v, slot, sorted_pos


def kernel(x_prompt, x_sample, state_lru_h, state_lru_conv, state_conf_conv, meta_tokens,
           norm_mix_g, w_in, conv_a_w, conv_a_b, w_r, b_r, w_i, b_i, lru_lambda,
           conv_b_w, conv_b_b, ln_b_g, ln_b_b, w_a_out, w_b_out, w_o, norm_ffn_g,
           w_router, b_router, w_gate, b_gate, w_up, b_up, w_down, b_down, norm_final_g):
    assert w_in.shape[0] == 1, "one layer"
    n_b, seq, d = x_prompt.shape
    n_s = x_sample.shape[0]
    n_meta = meta_tokens.shape[0]
    t_len = seq + n_meta
    n_p = n_b * t_len
    n_exp = w_router.shape[2]
    assert x_sample.shape[1] == 1 and n_s % _SUBLANES_BF16 == 0 and t_len % _SUBLANES_BF16 == 0
    assert d % _HEAD == 0 and w_r.shape[2] == _HEAD

    rows = _round_up(n_p + n_s, 256)
    tm_d = _largest_divisor(rows, 1056, 64)
    sub_d = _largest_divisor(tm_d, 704, _SUBLANES_BF16)
    tm_2 = _largest_divisor(rows, 704, 64)
    sub_2 = _largest_divisor(tm_2, 704, _SUBLANES_BF16)
    tn = _largest_divisor(d, 512, 128)
    tn_2 = _largest_divisor(d, 256, 128)
    tm_e, sub_e = (1280, 256) if rows >= 4096 else (128, 64)
    n_pairs = rows * _TOP_K
    n_chunks = n_exp + n_pairs // tm_e

    xp = jnp.concatenate(
        [jnp.broadcast_to(meta_tokens.astype(x_prompt.dtype)[None], (n_b, n_meta, d)), x_prompt],
        axis=1).reshape(n_p, d)
    x = jnp.concatenate([xp, x_sample.reshape(n_s, d),
                         jnp.zeros((rows - n_p - n_s, d), x_prompt.dtype)], axis=0)

    p = dict(conv_a_w=conv_a_w[0], conv_a_b=conv_a_b[0].reshape(1, d),
             w_r=w_r[0], b_r=b_r[0].reshape(1, d), w_i=w_i[0], b_i=b_i[0].reshape(1, d),
             lru_lambda=lru_lambda[0].reshape(1, d),
             conv_b_w=conv_b_w[0], conv_b_b=conv_b_b[0].reshape(1, d),
             ln_b_g=ln_b_g[0].reshape(1, d), ln_b_b=ln_b_b[0].reshape(1, d))

    xn = _rmsnorm_rows(x, norm_mix_g[0], _BF16)
    proj = _gmm([xn], [w_in], [(0, 0)], _dense_tables(rows, tm_d, sub_d),
                tm=tm_d, tn=tn, sub=sub_d, out_dtype=_F32,
                epilogue=lambda pr, ex: pr[0], name="in_proj")
    ya, h_p, ha_p = _branch_a_prompt(proj, p, n_seq=n_b, t_len=t_len, d=d, rows_total=rows)
    yb, hb_p = _branch_b_prompt(proj, p, n_seq=n_b, t_len=t_len, d=d, rows_total=rows)
    ya, yb, h_s, ha_s, hb_s = _branches_sample(
        proj, ya, yb, state_lru_h[0], state_lru_conv[0], state_conf_conv[0], p,
        row0=n_p, n_s=n_s, d=d, rows_total=rows)
    gate_off = 4 * d // tn_2
    merged = _gmm([ya, yb], [w_a_out, w_b_out], [(0, 0), (1, 1)],
                  _dense_tables(rows, tm_2, sub_2), tm=tm_2, tn=tn_2, sub=sub_2,
                  out_dtype=_BF16, name="merge_proj",
                  extras=[(proj, gate_off), (proj, gate_off + d // tn_2)],
                  epilogue=lambda pr, ex: _sigmoid(ex[0]) * pr[0] + _sigmoid(ex[1]) * pr[1])
    x = _gmm([merged], [w_o], [(0, 0)], _dense_tables(rows, tm_d, sub_d),
             tm=tm_d, tn=tn, sub=sub_d, out_dtype=_F32, name="out_proj",
             extras=[(x, 0)], epilogue=lambda pr, ex: ex[0] + pr[0])

    xn2, top_e, gates, rank, counts = _router(x, norm_ffn_g[0], w_router[0], b_router[0])
    tables, rows_v, last_v, first_v, slot, sorted_pos = _moe_tables(
        counts[0], top_e, rank, tm=tm_e, sub=sub_e, n_chunks=n_chunks)
    tok_ids = jnp.broadcast_to(jnp.arange(rows, dtype=jnp.int32)[:, None], (rows, _TOP_K))
    sorted_tok = jnp.zeros((n_pairs,), jnp.int32).at[sorted_pos.reshape(-1)].set(
        tok_ids.reshape(-1))
    xs = _dispatch_gather(xn2, sorted_tok, first_v, rows_v, last_v, tables[0], tm=tm_e, gb=sub_e)

    def swiglu(pr, ex):
        g = jnp.minimum(pr[0], _SWIGLU_LIMIT)
        u = jnp.clip(pr[1], -_SWIGLU_LIMIT, _SWIGLU_LIMIT)
        return g * _sigmoid(_SWIGLU_ALPHA * g) * (u + 1.0)

    hid = _gmm([xs], [w_gate[0], w_up[0]], [(0, 0), (0, 1)], tables, tm=tm_e, tn=tn_2,
               sub=sub_e, out_dtype=_BF16, name="expert_up", epilogue=swiglu,
               biases=[b_gate[0][:, None, :], b_up[0][:, None, :]])
    ys = _gmm([hid], [w_down[0]], [(0, 0)], tables, tm=tm_e, tn=tn, sub=sub_e,
              out_dtype=_F32, name="expert_down", epilogue=lambda pr, ex: pr[0],
              biases=[b_down[0][:, None, :]])
    y = _combine(ys, slot.reshape(-1).astype(jnp.int32), gates, x, norm_final_g)

    y_prompt = y[:n_p].reshape(n_b, t_len, d)[:, n_meta:]
    y_sample = y[n_p:n_p + n_s].reshape(n_s, 1, d)
    return (y_prompt, y_sample, h_p.reshape(1, n_b, d), ha_p[None], hb_p[None],
            h_s[None], ha_s[None], hb_s[None])
```

```python
import functools
import math

import jax
import jax.numpy as jnp
from jax import lax
from jax.experimental import pallas as pl
from jax.experimental.pallas import tpu as pltpu

_F32 = jnp.float32
_BF16 = jnp.bfloat16
_U32 = jnp.uint32

_EPS = 1e-6
_LRU_C = 8.0
_SWIGLU_LIMIT = 7.0
_SWIGLU_ALPHA = 1.702
_TOP_K = 4
_HEAD = 256

_V7X_VMEM_BYTES = 64 * 1024 * 1024
_VMEM_CAP = _V7X_VMEM_BYTES - 5 * 1024 * 1024
_SUBLANES_BF16 = 16

_TT_A_CAP = 512
_TT_B_CAP = 64
_EXPERT_CHUNK = 1280
_EXPERT_SUB = 256
_EXPERT_QUANTUM = 64


def _vmem_limit(estimate_bytes):
    return int(min(max(estimate_bytes + (6 << 20), 24 << 20), _VMEM_CAP))


def _largest_divisor(n, cap, mult):
    best = None
    for d in range(mult, min(n, cap) + 1, mult):
        if n % d == 0:
            best = d
    assert best is not None, (n, cap, mult)
    return best


def _round_up(n, m):
    return (n + m - 1) // m * m


def _sigmoid(x):
    return jax.nn.sigmoid(x)


def _gelu_tanh(x):
    c = math.sqrt(2.0 / math.pi)
    return 0.5 * x * (1.0 + jnp.tanh(c * (x + 0.044715 * (x * x * x))))


def _expm1(x):
    u = jnp.exp(x)
    um1 = u - 1.0
    return jnp.where(um1 == 0.0, x, jnp.where(um1 == -1.0, -1.0, um1 * (x / jnp.log(u))))


def _softplus(x):
    return jnp.maximum(x, 0.0) + jnp.log1p(jnp.exp(-jnp.abs(x)))


def _rms(x, g):
    ms = jnp.mean(x * x, axis=-1, keepdims=True)
    return (x * lax.rsqrt(ms + _EPS)) * g


def _rmsnorm_kernel(xm_ref, xt_ref, g_ref, x_ref, o_ref, *, n_main_blocks):
    i = pl.program_id(0)

    def emit(x):
        x_ref[...] = x
        o_ref[...] = _rms(x, g_ref[...]).astype(o_ref.dtype)

    @pl.when(i < n_main_blocks)
    def _():
        emit(xm_ref[...])

    @pl.when(i >= n_main_blocks)
    def _():
        emit(xt_ref[...])


def _rmsnorm_rows(x_main, x_tail, g, tr):
    n_main, d = x_main.shape
    rows = n_main + x_tail.shape[0]
    nmb = n_main // tr
    return pl.pallas_call(
        functools.partial(_rmsnorm_kernel, n_main_blocks=nmb),
        out_shape=(jax.ShapeDtypeStruct((rows, d), _F32),
                   jax.ShapeDtypeStruct((rows, d), _BF16)),
        grid=(rows // tr,),
        in_specs=[pl.BlockSpec((tr, d), lambda i: (jnp.minimum(i, nmb - 1), 0)),
                  pl.BlockSpec((tr, d), lambda i: (jnp.maximum(i - nmb, 0), 0)),
                  pl.BlockSpec((1, d), lambda i: (0, 0))],
        out_specs=(pl.BlockSpec((tr, d), lambda i: (i, 0)),
                   pl.BlockSpec((tr, d), lambda i: (i, 0))),
        compiler_params=pltpu.CompilerParams(
            dimension_semantics=("arbitrary",),
            vmem_limit_bytes=_vmem_limit(10 * tr * d * 4)),
        name="rmsnorm_rows",
    )(x_main, x_tail, g.reshape(1, d))


def _gmm_kernel(blk_ref, eid_ref, nrow_ref, *refs, n_lhs, n_w, has_bias, n_extra, pairs,
                tm, tn, sub, quantum, cast_rows, epilogue, packed, fused):
    del blk_ref, eid_ref
    pos = 0
    lhs_refs = refs[pos:pos + n_lhs]; pos += n_lhs
    w_refs = refs[pos:pos + n_w]; pos += n_w
    if has_bias:
        b_refs = refs[pos:pos + n_w]; pos += n_w
    ex_refs = refs[pos:pos + n_extra]; pos += n_extra
    o_ref = refs[pos]; pos += 1
    n_wbf = 1 if fused else n_w
    wbf_refs = refs[pos:pos + n_wbf]; pos += n_wbf
    if packed:
        lhs_refs = (refs[pos],)
        packed_ref = refs[0]

    nrows = nrow_ref[pl.program_id(0)]

    @pl.when(nrows > 0)
    def _():
        k_dim = w_refs[0].shape[0]

        def cast_body(c, carry):
            r = pl.multiple_of(c * cast_rows, cast_rows)
            for wi, w_ref in enumerate(w_refs):
                tile = w_ref[pl.ds(r, cast_rows), :].astype(_BF16)
                if fused:
                    wbf_refs[0][pl.ds(r, cast_rows), wi * tn:(wi + 1) * tn] = tile
                else:
                    wbf_refs[wi][pl.ds(r, cast_rows), :] = tile
            return carry

        lax.fori_loop(0, k_dim // cast_rows, cast_body, 0)

        if quantum is not None:
            per = sub // quantum
            n_q = jnp.maximum((nrows + quantum - 1) // quantum, per)

        if packed:
            half = k_dim // 2

            @pl.when(pl.program_id(1) == 0)
            def _():
                def unpack_body(q, carry):
                    r = pl.multiple_of(q * quantum, quantum)
                    w = packed_ref[pl.ds(r, quantum), :]
                    lo = lax.bitcast_convert_type(w << jnp.uint32(16), _F32)
                    hi = lax.bitcast_convert_type(w & jnp.uint32(0xFFFF0000), _F32)
                    lhs_refs[0][pl.ds(r, quantum), 0:half] = lo.astype(_BF16)
                    lhs_refs[0][pl.ds(r, quantum), half:k_dim] = hi.astype(_BF16)
                    return carry

                lax.fori_loop(0, n_q, unpack_body, 0)

        def compute(r, n):
            if fused:
                p = jnp.dot(lhs_refs[0][pl.ds(r, n), :], wbf_refs[0][...],
                            preferred_element_type=_F32)
                prods = [p[:, wi * tn:(wi + 1) * tn] for wi in range(n_w)]
            else:
                prods = [jnp.dot(lhs_refs[li][pl.ds(r, n), :], wbf_refs[wi][...],
                                 preferred_element_type=_F32) for (li, wi) in pairs]
            if has_bias:
                prods = [p + b_refs[wi][...] for p, (_, wi) in zip(prods, pairs)]
            extras = [e[pl.ds(r, n), :] for e in ex_refs]
            o_ref[pl.ds(r, n), :] = epilogue(prods, extras).astype(o_ref.dtype)

        def sub_body(s, carry):
            compute(pl.multiple_of(s * sub, sub), sub)
            return carry

        if quantum is None:
            lax.fori_loop(0, tm // sub, sub_body, 0)
        else:
            n_full = n_q // per
            n_tail = n_q - n_full * per
            lax.fori_loop(0, jnp.where(n_tail > 0, n_full - 1, n_full), sub_body, 0)
            for tq in range(1, per):
                @pl.when(n_tail == tq)
                def _(tq=tq):
                    compute(pl.multiple_of((n_full - 1) * sub, sub), sub + tq * quantum)


def _gmm(lhs, ws, pairs, tables, *, tm, tn, sub, out_dtype, epilogue, name, quantum=None,
         biases=None, extras=(), packed=False, single_buffer_lhs=False):
    blk, eid, nrow = tables
    n_chunks = blk.shape[0]
    rows = lhs[0].shape[0]
    k_dim = ws[0].shape[1]
    n_out = ws[0].shape[2]
    nj = n_out // tn
    fused = len(lhs) == 1 and len(ws) > 1
    assert n_out % tn == 0 and tm % sub == 0 and rows % tm == 0
    assert quantum is None or sub % quantum == 0
    cast_rows = _largest_divisor(k_dim, 512, _SUBLANES_BF16)

    def jj(v, j, nrow_ref):
        return jnp.where(nrow_ref[v] > 0, j, nj - 1)

    lhs_mode = dict(pipeline_mode=pl.Buffered(1)) if single_buffer_lhs else {}
    in_specs = []
    for a in lhs:
        in_specs.append(pl.BlockSpec((tm, a.shape[1]), lambda v, j, b, e, n: (b[v], 0), **lhs_mode))
    for _ in ws:
        in_specs.append(pl.BlockSpec((None, k_dim, tn),
                                     lambda v, j, b, e, n: (e[v], 0, jj(v, j, n))))
    has_bias = biases is not None
    if has_bias:
        for _ in ws:
            in_specs.append(pl.BlockSpec((None, 1, tn),
                                         lambda v, j, b, e, n: (e[v], 0, jj(v, j, n))))
    for (_, off) in extras:
        in_specs.append(pl.BlockSpec(
            (tm, tn), lambda v, j, b, e, n, off=off: (b[v], off + jj(v, j, n))))
    out_spec = pl.BlockSpec((tm, tn), lambda v, j, b, e, n: (b[v], jj(v, j, n)))

    scratch = ([pltpu.VMEM((k_dim, len(ws) * tn), _BF16)] if fused
               else [pltpu.VMEM((k_dim, tn), _BF16) for _ in ws])
    if packed:
        scratch.append(pltpu.VMEM((tm, k_dim), _BF16))
    out_bytes = jnp.dtype(out_dtype).itemsize
    lhs_bufs = 1 if single_buffer_lhs else 2
    est = (lhs_bufs * len(lhs) * tm * k_dim * 2 + (tm * k_dim * 2 if packed else 0)
           + len(ws) * (2 * k_dim * tn * 4 + k_dim * tn * 2)
           + 2 * len(extras) * tm * tn * 4 + 2 * tm * tn * out_bytes
           + (3 + len(pairs)) * sub * tn * 4)
    kernel = functools.partial(
        _gmm_kernel, n_lhs=len(lhs), n_w=len(ws), has_bias=has_bias, n_extra=len(extras),
        pairs=tuple(pairs), tm=tm, tn=tn, sub=sub, quantum=quantum, cast_rows=cast_rows,
        epilogue=epilogue, packed=packed, fused=fused)
    args = list(lhs) + list(ws) + (list(biases) if has_bias else []) + [a for (a, _) in extras]
    return pl.pallas_call(
        kernel,
        out_shape=jax.ShapeDtypeStruct((rows, n_out), out_dtype),
        grid_spec=pltpu.PrefetchScalarGridSpec(
            num_scalar_prefetch=3,
            grid=(n_chunks, nj),
            in_specs=in_specs,
            out_specs=out_spec,
            scratch_shapes=scratch),
        compiler_params=pltpu.CompilerParams(
            dimension_semantics=("arbitrary", "arbitrary"),
            vmem_limit_bytes=_vmem_limit(est)),
        name=name,
    )(blk, eid, nrow, *args)


def _dense_tables(rows, tm):
    n = rows // tm
    return (jnp.arange(n, dtype=jnp.int32), jnp.zeros((n,), jnp.int32),
            jnp.full((n,), tm, jnp.int32))


def _lru_gates(ca, wr_ref, br_ref, wi_ref, bi_ref, lam_ref):
    cab = ca.astype(_BF16)
    n_heads = wr_ref.shape[0]
    zr, zi = [], []
    for hh in range(n_heads):
        c_h = cab[:, hh * _HEAD:(hh + 1) * _HEAD]
        zr.append(jnp.dot(c_h, wr_ref[hh].astype(_BF16), preferred_element_type=_F32))
        zi.append(jnp.dot(c_h, wi_ref[hh].astype(_BF16), preferred_element_type=_F32))
    zr = jnp.concatenate(zr, axis=-1) if n_heads > 1 else zr[0]
    zi = jnp.concatenate(zi, axis=-1) if n_heads > 1 else zi[0]
    r = _sigmoid(zr + br_ref[...])
    i = _sigmoid(zi + bi_ref[...])
    log_a = (-_LRU_C * r) * _softplus(-lam_ref[...])
    a = jnp.exp(log_a)
    u = jnp.sqrt(-_expm1(2.0 * log_a)) * (i * ca)
    return a, u


def _branch_a_kernel(xm_ref, xa_ref, ga_ref, cw_ref, cb_ref, wr_ref, br_ref, wi_ref, bi_ref,
                     lam_ref, ya_ref, h_ref, hist_ref, buf, hcar, *, tt, kw, n_meta):
    t = pl.program_id(2)
    nt = pl.num_programs(2)
    hb = 8

    def run(x_tile, n):
        buf[hb:hb + n, :] = x_tile
        first = hb - (kw - 1)
        ca = cb_ref[...] + cw_ref[0:1, :] * buf[first:first + n, :]
        for k in range(1, kw):
            ca = ca + cw_ref[k:k + 1, :] * buf[first + k:first + k + n, :]
        tail = buf[hb + n - (kw - 1):hb + n, :]
        buf[first:hb, :] = tail
        a, u = _lru_gates(ca, wr_ref, br_ref, wi_ref, bi_ref, lam_ref)
        row = lax.broadcasted_iota(jnp.int32, a.shape, 0)
        d = 1
        while d < n:
            keep = row >= d
            a_sh = pltpu.roll(a, d, 0)
            u_sh = pltpu.roll(u, d, 0)
            u = jnp.where(keep, a * u_sh + u, u)
            a = jnp.where(keep, a * a_sh, a)
            d *= 2
        h = u + a * hcar[...]
        hcar[...] = h[n - 1:n, :]
        return h, tail

    @pl.when(t == 0)
    def _():
        buf[0:hb, :] = jnp.zeros((hb, buf.shape[1]), _F32)
        hcar[...] = jnp.zeros_like(hcar)
        run(xm_ref[...], n_meta)

    h, tail = run(xa_ref[...], tt)
    ya_ref[...] = (h * _gelu_tanh(ga_ref[...])).astype(ya_ref.dtype)

    @pl.when(t == nt - 1)
    def _():
        h_ref[...] = h[tt - 1:tt, :]
        hist_ref[...] = tail


def _branch_a_prompt(proj, p, *, n_seq, t_len, d, rows_total, meta_row0, n_meta):
    kw = p["conv_a_w"].shape[0]
    tt = _largest_divisor(t_len, _TT_A_CAP, _SUBLANES_BF16)
    cw = _largest_divisor(d, 512, _HEAD)
    nt, nc = t_len // tt, d // cw
    hpc = cw // _HEAD
    ga_off = d // cw
    mblk = meta_row0 // n_meta
    vec = lambda b, c, t: (0, c)
    kernel = functools.partial(_branch_a_kernel, tt=tt, kw=kw, n_meta=n_meta)
    return pl.pallas_call(
        kernel,
        out_shape=(jax.ShapeDtypeStruct((rows_total, d), _BF16),
                   jax.ShapeDtypeStruct((n_seq, 1, d), _F32),
                   jax.ShapeDtypeStruct((n_seq, kw - 1, d), _F32)),
        grid=(n_seq, nc, nt),
        in_specs=[
            pl.BlockSpec((n_meta, cw), lambda b, c, t: (mblk, c)),
            pl.BlockSpec((tt, cw), lambda b, c, t: (b * nt + t, c)),
            pl.BlockSpec((tt, cw), lambda b, c, t: (b * nt + t, ga_off + c)),
            pl.BlockSpec((kw, cw), vec),
            pl.BlockSpec((1, cw), vec),
            pl.BlockSpec((hpc, _HEAD, _HEAD), lambda b, c, t: (c, 0, 0)),
            pl.BlockSpec((1, cw), vec),
            pl.BlockSpec((hpc, _HEAD, _HEAD), lambda b, c, t: (c, 0, 0)),
            pl.BlockSpec((1, cw), vec),
            pl.BlockSpec((1, cw), vec),
        ],
        out_specs=(
            pl.BlockSpec((tt, cw), lambda b, c, t: (b * nt + t, c)),
            pl.BlockSpec((None, 1, cw), lambda b, c, t: (b, 0, c)),
            pl.BlockSpec((None, kw - 1, cw), lambda b, c, t: (b, 0, c)),
        ),
        scratch_shapes=[pltpu.VMEM((8 + max(tt, n_meta), cw), _F32), pltpu.VMEM((1, cw), _F32)],
        compiler_params=pltpu.CompilerParams(
            dimension_semantics=("arbitrary", "arbitrary", "arbitrary"),
            vmem_limit_bytes=_vmem_limit(24 * tt * cw * 4)),
        name="branch_a_prompt",
    )(proj, proj, proj, p["conv_a_w"], p["conv_a_b"], p["w_r"], p["b_r"], p["w_i"], p["b_i"],
      p["lru_lambda"])


def _layernorm_silu(cb, g, b):
    mu = jnp.mean(cb, axis=-1, keepdims=True)
    xc = cb - mu
    y = xc * lax.rsqrt(jnp.mean(xc * xc, axis=-1, keepdims=True) + _EPS)
    y = y * g + b
    return y * _sigmoid(y)


def _branch_b_kernel(xgm_ref, gtm_ref, xg_ref, gt_ref, cw_ref, cb_ref, lg_ref, lb_ref,
                     yb_ref, hist_ref, buf, cbuf, *, tt, kw, lane_chunk, n_meta):
    t = pl.program_id(1)
    nt = pl.num_programs(1)
    hb = _round_up(kw - 1, 8)
    d = buf.shape[1]

    @pl.when(t == 0)
    def _():
        buf[0:hb, :] = jnp.zeros((hb, d), _F32)
        buf[hb - n_meta:hb, :] = xgm_ref[...] * _sigmoid(gtm_ref[...])

    buf[hb:hb + tt, :] = xg_ref[...] * _sigmoid(gt_ref[...])
    first = hb - (kw - 1)
    for c0 in range(0, d, lane_chunk):
        cs = slice(c0, c0 + lane_chunk)
        acc = cb_ref[:, cs] + cw_ref[0:1, cs] * buf[first:first + tt, cs]
        for k in range(1, kw):
            acc = acc + cw_ref[k:k + 1, cs] * buf[first + k:first + k + tt, cs]
        cbuf[:, cs] = acc
    tail = buf[hb + tt - (kw - 1):hb + tt, :]
    buf[first:hb, :] = tail
    yb_ref[...] = _layernorm_silu(cbuf[...], lg_ref[...], lb_ref[...]).astype(yb_ref.dtype)

    @pl.when(t == nt - 1)
    def _():
        hist_ref[...] = tail


def _branch_b_prompt(proj, p, *, n_seq, t_len, d, rows_total, meta_row0, n_meta):
    kw = p["conv_b_w"].shape[0]
    tt = _largest_divisor(t_len, _TT_B_CAP, _SUBLANES_BF16)
    nt = t_len // tt
    hb = _round_up(kw - 1, 8)
    assert n_meta <= hb and tt >= kw - 1
    mblk = meta_row0 // n_meta
    vec = lambda b, t: (0, 0)
    kernel = functools.partial(_branch_b_kernel, tt=tt, kw=kw, n_meta=n_meta,
                               lane_chunk=_largest_divisor(d, 512, 128))
    return pl.pallas_call(
        kernel,
        out_shape=(jax.ShapeDtypeStruct((rows_total, d), _BF16),
                   jax.ShapeDtypeStruct((n_seq, kw - 1, d), _F32)),
        grid=(n_seq, nt),
        in_specs=[
            pl.BlockSpec((n_meta, d), lambda b, t: (mblk, 2)),
            pl.BlockSpec((n_meta, d), lambda b, t: (mblk, 3)),
            pl.BlockSpec((tt, d), lambda b, t: (b * nt + t, 2)),
            pl.BlockSpec((tt, d), lambda b, t: (b * nt + t, 3)),
            pl.BlockSpec((kw, d), vec),
            pl.BlockSpec((1, d), vec),
            pl.BlockSpec((1, d), vec),
            pl.BlockSpec((1, d), vec),
        ],
        out_specs=(
            pl.BlockSpec((tt, d), lambda b, t: (b * nt + t, 0)),
            pl.BlockSpec((None, kw - 1, d), lambda b, t: (b, 0, 0)),
        ),
        scratch_shapes=[pltpu.VMEM((hb + tt, d), _F32), pltpu.VMEM((tt, d), _F32)],
        compiler_params=pltpu.CompilerParams(
            dimension_semantics=("arbitrary", "arbitrary"),
            vmem_limit_bytes=_vmem_limit(16 * (hb + tt) * d * 4)),
        name="branch_b_prompt",
    )(proj, proj, proj, proj, p["conv_b_w"], p["conv_b_b"], p["ln_b_g"], p["ln_b_b"])


def _sample_kernel(xa_ref, ga_ref, xg_ref, gt_ref, sh_ref, sa_ref, sb_ref,
                   caw_ref, cab_ref, wr_ref, br_ref, wi_ref, bi_ref, lam_ref,
                   cbw_ref, cbb_ref, lg_ref, lb_ref, ya_in, yb_in,
                   ya_ref, yb_ref, oh_ref, oa_ref, ob_ref, glu_s, ca_s, cb_s,
                   *, tr, n_real, kwa, kwb):
    del ya_in, yb_in
    i = pl.program_id(0)

    @pl.when(i < n_real)
    def _():
        glu_s[...] = xg_ref[...] * _sigmoid(gt_ref[...])

        def row_body(n, carry):
            hb = sb_ref[n]
            g_row = glu_s[pl.ds(n, 1), :]
            cb_s[pl.ds(n, 1), :] = (
                jnp.sum(hb * cbw_ref[0:kwb - 1, :], axis=0, keepdims=True)
                + cbw_ref[kwb - 1:kwb, :] * g_row + cbb_ref[...])
            ob_ref[n, 0:kwb - 2, :] = hb[1:kwb - 1, :]
            ob_ref[n, kwb - 2:kwb - 1, :] = g_row
            ha = sa_ref[n]
            x_row = xa_ref[pl.ds(n, 1), :]
            ca_s[pl.ds(n, 1), :] = (
                jnp.sum(ha * caw_ref[0:kwa - 1, :], axis=0, keepdims=True)
                + caw_ref[kwa - 1:kwa, :] * x_row + cab_ref[...])
            oa_ref[n, 0:kwa - 2, :] = ha[1:kwa - 1, :]
            oa_ref[n, kwa - 2:kwa - 1, :] = x_row
            return carry

        lax.fori_loop(0, tr, row_body, 0)

        ca = ca_s[...]
        a, u = _lru_gates(ca, wr_ref, br_ref, wi_ref, bi_ref, lam_ref)
        h = a * sh_ref[...] + u
        oh_ref[...] = h
        ya_ref[...] = (h * _gelu_tanh(ga_ref[...])).astype(ya_ref.dtype)
        yb_ref[...] = _layernorm_silu(cb_s[...], lg_ref[...], lb_ref[...]).astype(yb_ref.dtype)

    @pl.when(i >= n_real)
    def _():
        ya_ref[...] = jnp.zeros_like(ya_ref)
        yb_ref[...] = jnp.zeros_like(yb_ref)


def _branches_sample(proj, ya, yb, state_h, state_a, state_b, p, *, row0, n_s, d, rows_total):
    tr = _SUBLANES_BF16
    kwa, kwb = p["conv_a_w"].shape[0], p["conv_b_w"].shape[0]
    n_real = n_s // tr
    n_blocks = (rows_total - row0) // tr
    b0 = row0 // tr
    n_heads = d // _HEAD
    full = lambda i: (0, 0)
    st = lambda i: (jnp.minimum(i, n_real - 1), 0)
    st3 = lambda i: (jnp.minimum(i, n_real - 1), 0, 0)
    kernel = functools.partial(_sample_kernel, tr=tr, n_real=n_real, kwa=kwa, kwb=kwb)
    col = lambda c: pl.BlockSpec((tr, d), lambda i, c=c: (b0 + i, c))
    est = 4 * tr * (_round_up(kwb - 1, 8) + 8) * d * 4 + 16 * tr * d * 4 + (kwb + 16) * d * 4
    return pl.pallas_call(
        kernel,
        out_shape=(jax.ShapeDtypeStruct((rows_total, d), _BF16),
                   jax.ShapeDtypeStruct((rows_total, d), _BF16),
                   jax.ShapeDtypeStruct((n_s, d), _F32),
                   jax.ShapeDtypeStruct((n_s, kwa - 1, d), _F32),
                   jax.ShapeDtypeStruct((n_s, kwb - 1, d), _F32)),
        grid=(n_blocks,),
        in_specs=[
            col(0), col(1), col(2), col(3),
            pl.BlockSpec((tr, d), st),
            pl.BlockSpec((tr, kwa - 1, d), st3),
            pl.BlockSpec((tr, kwb - 1, d), st3),
            pl.BlockSpec((kwa, d), full), pl.BlockSpec((1, d), full),
            pl.BlockSpec((n_heads, _HEAD, _HEAD), lambda i: (0, 0, 0)), pl.BlockSpec((1, d), full),
            pl.BlockSpec((n_heads, _HEAD, _HEAD), lambda i: (0, 0, 0)), pl.BlockSpec((1, d), full),
            pl.BlockSpec((1, d), full),
            pl.BlockSpec((kwb, d), full), pl.BlockSpec((1, d), full),
            pl.BlockSpec((1, d), full), pl.BlockSpec((1, d), full),
            pl.BlockSpec(memory_space=pl.ANY), pl.BlockSpec(memory_space=pl.ANY),
        ],
        out_specs=(
            pl.BlockSpec((tr, d), lambda i: (b0 + i, 0)),
            pl.BlockSpec((tr, d), lambda i: (b0 + i, 0)),
            pl.BlockSpec((tr, d), st),
            pl.BlockSpec((tr, kwa - 1, d), st3),
            pl.BlockSpec((tr, kwb - 1, d), st3),
        ),
        scratch_shapes=[pltpu.VMEM((tr, d), _F32)] * 3,
        input_output_aliases={18: 0, 19: 1},
        compiler_params=pltpu.CompilerParams(
            dimension_semantics=("arbitrary",),
            vmem_limit_bytes=_vmem_limit(est)),
        name="branches_sample",
    )(proj, proj, proj, proj, state_h, state_a, state_b,
      p["conv_a_w"], p["conv_a_b"], p["w_r"], p["b_r"], p["w_i"], p["b_i"], p["lru_lambda"],
      p["conv_b_w"], p["conv_b_b"], p["ln_b_g"], p["ln_b_b"], ya, yb)


def _router_kernel(x_ref, g_ref, wr_ref, br_ref, xp_ref, te_ref, tg_ref, rk_ref, cnt_ref,
                   carry, *, n_exp):
    i = pl.program_id(0)

    @pl.when(i == 0)
    def _():
        carry[...] = jnp.zeros_like(carry)

    xn = _rms(x_ref[...], g_ref[...])
    xh = xn.astype(_BF16)
    xhf = xh.astype(_F32)
    half = xn.shape[1] // 2
    bits = lax.bitcast_convert_type(xhf, _U32)
    xp_ref[...] = (bits[:, 0:half] >> jnp.uint32(16)) | bits[:, half:2 * half]

    w = wr_ref[...]
    xl = (xn - xhf).astype(_BF16)
    wh = w.astype(_BF16)
    wl = (w - wh.astype(_F32)).astype(_BF16)
    logits = (jnp.dot(xh, wh, preferred_element_type=_F32)
              + (jnp.dot(xl, wh, preferred_element_type=_F32)
                 + jnp.dot(xh, wl, preferred_element_type=_F32))) + br_ref[...]

    tb = xn.shape[0]
    lane = lax.broadcasted_iota(jnp.int32, (tb, n_exp), 1)
    work = logits
    vals, idxs, hots = [], [], []
    for _ in range(_TOP_K):
        m = jnp.max(work, axis=-1, keepdims=True)
        idx = jnp.min(jnp.where(work == m, lane, n_exp), axis=-1, keepdims=True)
        hot = lane == idx
        vals.append(m); idxs.append(idx); hots.append(hot)
        work = jnp.where(hot, -jnp.inf, work)
    es = [jnp.exp(v - vals[0]) for v in vals]
    den = es[0] + es[1] + es[2] + es[3]
    tg_ref[...] = jnp.concatenate([e / den for e in es], axis=-1)
    te_ref[...] = jnp.concatenate(idxs, axis=-1)

    sel = (hots[0] | hots[1] | hots[2] | hots[3]).astype(_F32)
    r_i = lax.broadcasted_iota(jnp.int32, (tb, tb), 0)
    c_i = lax.broadcasted_iota(jnp.int32, (tb, tb), 1)
    tri = (c_i < r_i).astype(_BF16)
    before = jnp.dot(tri, sel.astype(_BF16), preferred_element_type=_F32) + carry[...]
    ranks = [jnp.sum(jnp.where(h, before, 0.0), axis=-1, keepdims=True) for h in hots]
    rk_ref[...] = jnp.concatenate(ranks, axis=-1).astype(jnp.int32)
    total = carry[...] + jnp.sum(sel, axis=0, keepdims=True)
    carry[...] = total
    cnt_ref[...] = total.astype(jnp.int32)


def _router(x2, g, w_router, b_router):
    rows, d = x2.shape
    n_exp = w_router.shape[1]
    tb = _largest_divisor(rows, 256, 8)
    tok = lambda i: (i, 0)
    full = lambda i: (0, 0)
    return pl.pallas_call(
        functools.partial(_router_kernel, n_exp=n_exp),
        out_shape=(jax.ShapeDtypeStruct((rows, d // 2), _U32),
                   jax.ShapeDtypeStruct((rows, _TOP_K), jnp.int32),
                   jax.ShapeDtypeStruct((rows, _TOP_K), _F32),
                   jax.ShapeDtypeStruct((rows, _TOP_K), jnp.int32),
                   jax.ShapeDtypeStruct((1, n_exp), jnp.int32)),
        grid=(rows // tb,),
        in_specs=[pl.BlockSpec((tb, d), tok), pl.BlockSpec((1, d), full),
                  pl.BlockSpec((d, n_exp), full), pl.BlockSpec((1, n_exp), full)],
        out_specs=(pl.BlockSpec((tb, d // 2), tok), pl.BlockSpec((tb, _TOP_K), tok),
                   pl.BlockSpec((tb, _TOP_K), tok), pl.BlockSpec((tb, _TOP_K), tok),
                   pl.BlockSpec((1, n_exp), full)),
        scratch_shapes=[pltpu.VMEM((1, n_exp), _F32)],
        compiler_params=pltpu.CompilerParams(
            dimension_semantics=("arbitrary",),
            vmem_limit_bytes=_vmem_limit(8 * tb * d * 4)),
        name="router",
    )(x2, g.reshape(1, d), w_router, b_router.reshape(1, n_exp))


def _hbm_row_copy(src, src_row, dst, dst_row, sem):
    return pltpu.make_async_copy(src.at[pl.ds(src_row, 1), :], dst.at[pl.ds(dst_row, 1), :], sem)


def _dispatch_kernel(slot_ref, padrow_ref, padn_ref, x_hbm, z_hbm, o_hbm, sem,
                     *, n_tok, blk, n_chunks):
    def issue_block(b):
        def body(t, carry):
            tok = b * blk + t
            for k in range(_TOP_K):
                _hbm_row_copy(x_hbm, tok, o_hbm, slot_ref[tok * _TOP_K + k], sem).start()
            return carry
        lax.fori_loop(0, blk, body, 0)

    def drain(n):
        def body(t, carry):
            _hbm_row_copy(x_hbm, 0, o_hbm, 0, sem).wait()
            return carry
        lax.fori_loop(0, n, body, 0)

    issue_block(0)

    def step(b, carry):
        issue_block(b)
        drain(blk * _TOP_K)
        return carry

    lax.fori_loop(1, n_tok // blk, step, 0)
    drain(blk * _TOP_K)

    def zero_fill(v, carry):
        n = padn_ref[v]

        def body(r, c2):
            _hbm_row_copy(z_hbm, 0, o_hbm, padrow_ref[v] + r, sem).start()
            return c2
        lax.fori_loop(0, n, body, 0)
        drain(n)
        return carry

    lax.fori_loop(0, n_chunks, zero_fill, 0)


def _dispatch(xp, slots, pad_row, pad_n, *, n_rows_out):
    n_tok, half = xp.shape
    blk = _largest_divisor(n_tok, 256, 8)
    zero_row = jnp.zeros((8, half), _U32)
    return pl.pallas_call(
        functools.partial(_dispatch_kernel, n_tok=n_tok, blk=blk, n_chunks=pad_n.shape[0]),
        out_shape=jax.ShapeDtypeStruct((n_rows_out, half), _U32),
        grid_spec=pltpu.PrefetchScalarGridSpec(
            num_scalar_prefetch=3,
            grid=(1,),
            in_specs=[pl.BlockSpec(memory_space=pl.ANY), pl.BlockSpec(memory_space=pl.ANY)],
            out_specs=pl.BlockSpec(memory_space=pl.ANY),
            scratch_shapes=[pltpu.SemaphoreType.DMA(())]),
        compiler_params=pltpu.CompilerParams(dimension_semantics=("arbitrary",)),
        name="dispatch",
    )(slots, pad_row, pad_n, xp, zero_row)


def _combine_kernel(slot_ref, ys_hbm, g_ref, x_ref, gf_ref, om_ref, ot_ref, buf, sem,
                    *, tb, n_main_blocks):
    i = pl.program_id(0)
    n = pl.num_programs(0)

    def copy(b, r, k, slot_row):
        par = b % 2
        return pltpu.make_async_copy(ys_hbm.at[pl.ds(slot_row, 1), :],
                                     buf.at[par, k, pl.ds(r, 1), :], sem.at[par])

    def issue(b):
        base = b * (tb * _TOP_K)

        def body(r, carry):
            for k in range(_TOP_K):
                copy(b, r, k, slot_ref[base + r * _TOP_K + k]).start()
            return carry
        lax.fori_loop(0, tb, body, 0)

    @pl.when(i == 0)
    def _():
        issue(i)

    @pl.when(i + 1 < n)
    def _():
        issue(i + 1)

    def drain(r, carry):
        for k in range(_TOP_K):
            copy(i, r, k, 0).wait()
        return carry
    lax.fori_loop(0, tb, drain, 0)

    par = i % 2
    g = g_ref[...]
    moe = g[:, 0:1] * buf[par, 0]
    for k in range(1, _TOP_K):
        moe = moe + g[:, k:k + 1] * buf[par, k]
    y = _rms(x_ref[...] + moe, gf_ref[...])

    @pl.when(i < n_main_blocks)
    def _():
        om_ref[...] = y

    @pl.when(i >= n_main_blocks)
    def _():
        ot_ref[...] = y


def _combine(ys, slots, gates, x2, g_final, *, n_main, tb):
    rows, d = x2.shape
    nmb = n_main // tb
    tok = lambda i, s: (i, 0)
    return pl.pallas_call(
        functools.partial(_combine_kernel, tb=tb, n_main_blocks=nmb),
        out_shape=(jax.ShapeDtypeStruct((n_main, d), _F32),
                   jax.ShapeDtypeStruct((rows - n_main, d), _F32)),
        grid_spec=pltpu.PrefetchScalarGridSpec(
            num_scalar_prefetch=1,
            grid=(rows // tb,),
            in_specs=[pl.BlockSpec(memory_space=pl.ANY),
                      pl.BlockSpec((tb, _TOP_K), tok),
                      pl.BlockSpec((tb, d), tok),
                      pl.BlockSpec((1, d), lambda i, s: (0, 0))],
            out_specs=(pl.BlockSpec((tb, d), lambda i, s: (jnp.minimum(i, nmb - 1), 0)),
                       pl.BlockSpec((tb, d), lambda i, s: (jnp.maximum(i - nmb, 0), 0))),
            scratch_shapes=[pltpu.VMEM((2, _TOP_K, tb, d), _F32),
                            pltpu.SemaphoreType.DMA((2,))]),
        compiler_params=pltpu.CompilerParams(
            dimension_semantics=("arbitrary",),
            vmem_limit_bytes=_vmem_limit((2 * _TOP_K + 10) * tb * d * 4)),
        name="combine",
    )(slots, ys, gates, x2, g_final.reshape(1, d))


def _moe_tables(counts, top_e, rank, *, tm, sub, quantum, n_chunks):
    n_exp = counts.shape[0]
    chunks_e = (counts + tm - 1) // tm
    chunk_end = jnp.cumsum(chunks_e)
    chunk_start = chunk_end - chunks_e
    n_used = chunk_end[-1]
    v = jnp.arange(n_chunks, dtype=jnp.int32)
    v_eff = jnp.minimum(v, n_used - 1)
    e_v = jnp.minimum(jnp.searchsorted(chunk_end, v_eff, side="right"), n_exp - 1).astype(jnp.int32)
    within = v_eff - chunk_start[e_v]
    rows_v = jnp.where(v < n_used, jnp.clip(counts[e_v] - within * tm, 0, tm), 0).astype(jnp.int32)
    pad_row = (v_eff * tm + rows_v).astype(jnp.int32)
    filled = jnp.maximum((rows_v + quantum - 1) // quantum * quantum, sub)
    pad_n = jnp.where(rows_v > 0, filled - rows_v, 0).astype(jnp.int32)
    slot = (chunk_start[top_e] * tm + rank).astype(jnp.int32)
    return (v_eff.astype(jnp.int32), e_v, rows_v), pad_row, pad_n, slot


def kernel(x_prompt, x_sample, state_lru_h, state_lru_conv, state_conf_conv, meta_tokens,
           norm_mix_g, w_in, conv_a_w, conv_a_b, w_r, b_r, w_i, b_i, lru_lambda,
           conv_b_w, conv_b_b, ln_b_g, ln_b_b, w_a_out, w_b_out, w_o, norm_ffn_g,
           w_router, b_router, w_gate, b_gate, w_up, b_up, w_down, b_down, norm_final_g):
    assert w_in.shape[0] == 1, "one layer"
    n_b, seq, d = x_prompt.shape
    n_s = x_sample.shape[0]
    n_meta = meta_tokens.shape[0]
    n_main = n_b * seq
    n_exp = w_router.shape[2]
    assert x_sample.shape[1] == 1 and n_s % _SUBLANES_BF16 == 0 and seq % _SUBLANES_BF16 == 0
    assert d % _HEAD == 0 and w_r.shape[2] == _HEAD
    assert n_meta % _SUBLANES_BF16 == 0 and (n_main + n_s) % n_meta == 0

    tr = _largest_divisor(n_main, 256, _SUBLANES_BF16)
    rows_tail = _round_up(n_s + n_meta, tr)
    rows = n_main + rows_tail
    meta_row0 = n_main + n_s
    tm_d = _largest_divisor(rows, 2112, 32)
    tm_o = _largest_divisor(rows, 1056, 32)
    sub_d = _largest_divisor(tm_d, 704, _SUBLANES_BF16)
    sub_o = _largest_divisor(tm_o, 704, _SUBLANES_BF16)
    tm_2 = _largest_divisor(rows, 1408, 32)
    sub_2 = _largest_divisor(tm_2, 704, _SUBLANES_BF16)
    tn = _largest_divisor(d, 512, 128)
    tn_2 = _largest_divisor(d, 256, 128)
    big = rows >= 4096
    tm_e = _EXPERT_CHUNK if big else 128
    sub_e = _EXPERT_SUB if big else 64
    q_e = _EXPERT_QUANTUM if big else 16
    n_pairs = rows * _TOP_K
    n_chunks = n_exp + n_pairs // tm_e

    x_tail = jnp.concatenate([x_sample.reshape(n_s, d), meta_tokens.astype(x_prompt.dtype),
                              jnp.zeros((rows_tail - n_s - n_meta, d), x_prompt.dtype)], axis=0)

    p = dict(conv_a_w=conv_a_w[0], conv_a_b=conv_a_b[0].reshape(1, d),
             w_r=w_r[0], b_r=b_r[0].reshape(1, d), w_i=w_i[0], b_i=b_i[0].reshape(1, d),
             lru_lambda=lru_lambda[0].reshape(1, d),
             conv_b_w=conv_b_w[0], conv_b_b=conv_b_b[0].reshape(1, d),
             ln_b_g=ln_b_g[0].reshape(1, d), ln_b_b=ln_b_b[0].reshape(1, d))

    x, xn = _rmsnorm_rows(x_prompt.reshape(n_main, d), x_tail, norm_mix_g[0], tr)
    proj = _gmm([xn], [w_in], [(0, 0)], _dense_tables(rows, tm_d),
                tm=tm_d, tn=tn, sub=sub_d, out_dtype=_F32, single_buffer_lhs=True,
                epilogue=lambda pr, ex: pr[0], name="in_proj")
    seq_args = dict(n_seq=n_b, t_len=seq, d=d, rows_total=rows, meta_row0=meta_row0, n_meta=n_meta)
    ya, h_p, ha_p = _branch_a_prompt(proj, p, **seq_args)
    yb, hb_p = _branch_b_prompt(proj, p, **seq_args)
    ya, yb, h_s, ha_s, hb_s = _branches_sample(
        proj, ya, yb, state_lru_h[0], state_lru_conv[0], state_conf_conv[0], p,
        row0=n_main, n_s=n_s, d=d, rows_total=rows)
    gate_off = 4 * d // tn_2
    merged = _gmm([ya, yb], [w_a_out, w_b_out], [(0, 0), (1, 1)],
                  _dense_tables(rows, tm_2), tm=tm_2, tn=tn_2, sub=sub_2,
                  out_dtype=_BF16, name="merge_proj", single_buffer_lhs=True,
                  extras=[(proj, gate_off), (proj, gate_off + d // tn_2)],
                  epilogue=lambda pr, ex: _sigmoid(ex[0]) * pr[0] + _sigmoid(ex[1]) * pr[1])
    x2 = _gmm([merged], [w_o], [(0, 0)], _dense_tables(rows, tm_o),
              tm=tm_o, tn=tn, sub=sub_o, out_dtype=_F32, name="out_proj",
              extras=[(x, 0)], epilogue=lambda pr, ex: ex[0] + pr[0])

    xp, top_e, gates, rank, counts = _router(x2, norm_ffn_g[0], w_router[0], b_router[0])
    tables, pad_row, pad_n, slot = _moe_tables(
        counts[0], top_e, rank, tm=tm_e, sub=sub_e, quantum=q_e, n_chunks=n_chunks)
    slots = slot.reshape(-1)
    xs = _dispatch(xp, slots, pad_row, pad_n, n_rows_out=n_chunks * tm_e)

    def swiglu(pr, ex):
        g = jnp.minimum(pr[0], _SWIGLU_LIMIT)
        u = jnp.clip(pr[1], -_SWIGLU_LIMIT, _SWIGLU_LIMIT)
        return g * _sigmoid(_SWIGLU_ALPHA * g) * (u + 1.0)

    hid = _gmm([xs], [w_gate[0], w_up[0]], [(0, 0), (0, 1)], tables, tm=tm_e, tn=tn_2,
               sub=sub_e, quantum=q_e, out_dtype=_BF16, name="expert_up", epilogue=swiglu,
               packed=True,
               biases=[b_gate[0][:, None, :], b_up[0][:, None, :]])
    ys = _gmm([hid], [w_down[0]], [(0, 0)], tables, tm=tm_e, tn=tn, sub=sub_e, quantum=q_e,
              out_dtype=_F32, name="expert_down", epilogue=lambda pr, ex: pr[0],
              biases=[b_down[0][:, None, :]])
    tb_c = _largest_divisor(math.gcd(n_main, rows_tail), 64, 8)
    y_main, y_tail = _combine(ys, slots, gates, x2, norm_final_g, n_main=n_main, tb=tb_c)

    return (y_main.reshape(n_b, seq, d), y_tail[:n_s].reshape(n_s, 1, d),
            h_p.reshape(1, n_b, d), ha_p[None], hb_p[None], h_s[None], ha_s[None], hb_s[None])
```

```python
import functools
import math

import jax
import jax.numpy as jnp
from jax import lax
from jax.experimental import pallas as pl
from jax.experimental.pallas import tpu as pltpu

_F32 = jnp.float32
_BF16 = jnp.bfloat16
_U32 = jnp.uint32

_EPS = 1e-6
_LRU_C = 8.0
_SWIGLU_LIMIT = 7.0
_SWIGLU_ALPHA = 1.702
_TOP_K = 4
_HEAD = 256

_V7X_VMEM_BYTES = 64 * 1024 * 1024
_VMEM_CAP = _V7X_VMEM_BYTES - 5 * 1024 * 1024
_SUBLANES_BF16 = 16
_SCAN_GROUP = 8

_TT_A_CAP = 512
_TT_B_CAP = 64
_EXPERT_CHUNK = 1280
_EXPERT_SUB = 256
_EXPERT_QUANTUM = 64


def _vmem_limit(estimate_bytes):
    return int(min(max(estimate_bytes + (6 << 20), 24 << 20), _VMEM_CAP))


def _largest_divisor(n, cap, mult):
    best = None
    for d in range(mult, min(n, cap) + 1, mult):
        if n % d == 0:
            best = d
    assert best is not None, (n, cap, mult)
    return best


def _round_up(n, m):
    return (n + m - 1) // m * m


def _sigmoid(x):
    return jax.nn.sigmoid(x)


def _gelu_tanh(x):
    c = math.sqrt(2.0 / math.pi)
    return 0.5 * x * (1.0 + jnp.tanh(c * (x + 0.044715 * (x * x * x))))


def _softplus(x):
    return jnp.maximum(x, 0.0) + jnp.log1p(jnp.exp(-jnp.abs(x)))


def _rms(x, g):
    ms = jnp.mean(x * x, axis=-1, keepdims=True)
    return (x * lax.rsqrt(ms + _EPS)) * g


def _rmsnorm_kernel(xm_ref, xt_ref, g_ref, x_ref, o_ref, *, n_main_blocks):
    i = pl.program_id(0)

    def emit(x):
        x_ref[...] = x
        o_ref[...] = _rms(x, g_ref[...]).astype(o_ref.dtype)

    @pl.when(i < n_main_blocks)
    def _():
        emit(xm_ref[...])

    @pl.when(i >= n_main_blocks)
    def _():
        emit(xt_ref[...])


def _rmsnorm_rows(x_main, x_tail, g, tr):
    n_main, d = x_main.shape
    rows = n_main + x_tail.shape[0]
    nmb = n_main // tr
    return pl.pallas_call(
        functools.partial(_rmsnorm_kernel, n_main_blocks=nmb),
        out_shape=(jax.ShapeDtypeStruct((rows, d), _F32),
                   jax.ShapeDtypeStruct((rows, d), _BF16)),
        grid=(rows // tr,),
        in_specs=[pl.BlockSpec((tr, d), lambda i: (jnp.minimum(i, nmb - 1), 0)),
                  pl.BlockSpec((tr, d), lambda i: (jnp.maximum(i - nmb, 0), 0)),
                  pl.BlockSpec((1, d), lambda i: (0, 0))],
        out_specs=(pl.BlockSpec((tr, d), lambda i: (i, 0)),
                   pl.BlockSpec((tr, d), lambda i: (i, 0))),
        compiler_params=pltpu.CompilerParams(
            dimension_semantics=("arbitrary",),
            vmem_limit_bytes=_vmem_limit(10 * tr * d * 4)),
        name="rmsnorm_rows",
    )(x_main, x_tail, g.reshape(1, d))


def _gmm_kernel(blk_ref, eid_ref, nrow_ref, *refs, n_lhs, n_w, has_bias, n_extra, pairs,
                tm, tn, sub, quantum, cast_rows, epilogue, packed, fused):
    del blk_ref, eid_ref
    pos = 0
    lhs_refs = refs[pos:pos + n_lhs]; pos += n_lhs
    w_refs = refs[pos:pos + n_w]; pos += n_w
    if has_bias:
        b_refs = refs[pos:pos + n_w]; pos += n_w
    ex_refs = refs[pos:pos + n_extra]; pos += n_extra
    o_ref = refs[pos]; pos += 1
    n_wbf = 1 if fused else n_w
    wbf_refs = refs[pos:pos + n_wbf]; pos += n_wbf
    if packed:
        lhs_refs = (refs[pos],)
        packed_ref = refs[0]

    nrows = nrow_ref[pl.program_id(0)]

    @pl.when(nrows > 0)
    def _():
        k_dim = w_refs[0].shape[0]

        def cast_body(c, carry):
            r = pl.multiple_of(c * cast_rows, cast_rows)
            for wi, w_ref in enumerate(w_refs):
                tile = w_ref[pl.ds(r, cast_rows), :].astype(_BF16)
                if fused:
                    wbf_refs[0][pl.ds(r, cast_rows), wi * tn:(wi + 1) * tn] = tile
                else:
                    wbf_refs[wi][pl.ds(r, cast_rows), :] = tile
            return carry

        lax.fori_loop(0, k_dim // cast_rows, cast_body, 0)

        if quantum is not None:
            per = sub // quantum
            n_q = jnp.maximum((nrows + quantum - 1) // quantum, per)

        if packed:
            half = k_dim // 2

            @pl.when(pl.program_id(1) == 0)
            def _():
                def unpack_body(q, carry):
                    r = pl.multiple_of(q * quantum, quantum)
                    w = packed_ref[pl.ds(r, quantum), :]
                    lo = lax.bitcast_convert_type(w << jnp.uint32(16), _F32)
                    hi = lax.bitcast_convert_type(w & jnp.uint32(0xFFFF0000), _F32)
                    lhs_refs[0][pl.ds(r, quantum), 0:half] = lo.astype(_BF16)
                    lhs_refs[0][pl.ds(r, quantum), half:k_dim] = hi.astype(_BF16)
                    return carry

                lax.fori_loop(0, n_q, unpack_body, 0)

        def compute(r, n):
            if fused:
                p = jnp.dot(lhs_refs[0][pl.ds(r, n), :], wbf_refs[0][...],
                            preferred_element_type=_F32)
                prods = [p[:, wi * tn:(wi + 1) * tn] for wi in range(n_w)]
            else:
                prods = [jnp.dot(lhs_refs[li][pl.ds(r, n), :], wbf_refs[wi][...],
                                 preferred_element_type=_F32) for (li, wi) in pairs]
            if has_bias:
                prods = [p + b_refs[wi][...] for p, (_, wi) in zip(prods, pairs)]
            extras = [e[pl.ds(r, n), :] for e in ex_refs]
            o_ref[pl.ds(r, n), :] = epilogue(prods, extras).astype(o_ref.dtype)

        def sub_body(s, carry):
            compute(pl.multiple_of(s * sub, sub), sub)
            return carry

        if quantum is None:
            lax.fori_loop(0, tm // sub, sub_body, 0)
        else:
            n_full = n_q // per
            n_tail = n_q - n_full * per
            lax.fori_loop(0, jnp.where(n_tail > 0, n_full - 1, n_full), sub_body, 0)
            for tq in range(1, per):
                @pl.when(n_tail == tq)
                def _(tq=tq):
                    compute(pl.multiple_of((n_full - 1) * sub, sub), sub + tq * quantum)


def _gmm(lhs, ws, pairs, tables, *, tm, tn, sub, out_dtype, epilogue, name, quantum=None,
         biases=None, extras=(), packed=False, single_buffer_lhs=False):
    blk, eid, nrow = tables
    n_chunks = blk.shape[0]
    rows = lhs[0].shape[0]
    k_dim = ws[0].shape[1]
    n_out = ws[0].shape[2]
    nj = n_out // tn
    fused = len(lhs) == 1 and len(ws) > 1
    assert n_out % tn == 0 and tm % sub == 0 and rows % tm == 0
    assert quantum is None or sub % quantum == 0
    cast_rows = _largest_divisor(k_dim, 512, _SUBLANES_BF16)

    def jj(v, j, nrow_ref):
        return jnp.where(nrow_ref[v] > 0, j, nj - 1)

    lhs_mode = dict(pipeline_mode=pl.Buffered(1)) if single_buffer_lhs else {}
    in_specs = []
    for a in lhs:
        in_specs.append(pl.BlockSpec((tm, a.shape[1]), lambda v, j, b, e, n: (b[v], 0), **lhs_mode))
    for _ in ws:
        in_specs.append(pl.BlockSpec((None, k_dim, tn),
                                     lambda v, j, b, e, n: (e[v], 0, jj(v, j, n))))
    has_bias = biases is not None
    if has_bias:
        for _ in ws:
            in_specs.append(pl.BlockSpec((None, 1, tn),
                                         lambda v, j, b, e, n: (e[v], 0, jj(v, j, n))))
    for (_, off) in extras:
        in_specs.append(pl.BlockSpec(
            (tm, tn), lambda v, j, b, e, n, off=off: (b[v], off + jj(v, j, n))))
    out_spec = pl.BlockSpec((tm, tn), lambda v, j, b, e, n: (b[v], jj(v, j, n)))

    scratch = ([pltpu.VMEM((k_dim, len(ws) * tn), _BF16)] if fused
               else [pltpu.VMEM((k_dim, tn), _BF16) for _ in ws])
    if packed:
        scratch.append(pltpu.VMEM((tm, k_dim), _BF16))
    out_bytes = jnp.dtype(out_dtype).itemsize
    lhs_bufs = 1 if single_buffer_lhs else 2
    est = (lhs_bufs * len(lhs) * tm * k_dim * 2 + (tm * k_dim * 2 if packed else 0)
           + len(ws) * (2 * k_dim * tn * 4 + k_dim * tn * 2)
           + 2 * len(extras) * tm * tn * 4 + 2 * tm * tn * out_bytes
           + (3 + len(pairs)) * sub * tn * 4)
    kernel = functools.partial(
        _gmm_kernel, n_lhs=len(lhs), n_w=len(ws), has_bias=has_bias, n_extra=len(extras),
        pairs=tuple(pairs), tm=tm, tn=tn, sub=sub, quantum=quantum, cast_rows=cast_rows,
        epilogue=epilogue, packed=packed, fused=fused)
    args = list(lhs) + list(ws) + (list(biases) if has_bias else []) + [a for (a, _) in extras]
    return pl.pallas_call(
        kernel,
        out_shape=jax.ShapeDtypeStruct((rows, n_out), out_dtype),
        grid_spec=pltpu.PrefetchScalarGridSpec(
            num_scalar_prefetch=3,
            grid=(n_chunks, nj),
            in_specs=in_specs,
            out_specs=out_spec,
            scratch_shapes=scratch),
        compiler_params=pltpu.CompilerParams(
            dimension_semantics=("arbitrary", "arbitrary"),
            vmem_limit_bytes=_vmem_limit(est)),
        name=name,
    )(blk, eid, nrow, *args)


def _dense_tables(rows, tm):
    n = rows // tm
    return (jnp.arange(n, dtype=jnp.int32), jnp.zeros((n,), jnp.int32),
            jnp.full((n,), tm, jnp.int32))


def _lru_gates(ca, wr_ref, br_ref, wi_ref, bi_ref, lam_ref):
    cab = ca.astype(_BF16)
    n_heads = wr_ref.shape[0]
    zr, zi = [], []
    for hh in range(n_heads):
        c_h = cab[:, hh * _HEAD:(hh + 1) * _HEAD]
        zr.append(jnp.dot(c_h, wr_ref[hh].astype(_BF16), preferred_element_type=_F32))
        zi.append(jnp.dot(c_h, wi_ref[hh].astype(_BF16), preferred_element_type=_F32))
    zr = jnp.concatenate(zr, axis=-1) if n_heads > 1 else zr[0]
    zi = jnp.concatenate(zi, axis=-1) if n_heads > 1 else zi[0]
    r = _sigmoid(zr + br_ref[...])
    i = _sigmoid(zi + bi_ref[...])
    log_a = (-_LRU_C * r) * _softplus(-lam_ref[...])
    a = jnp.exp(log_a)
    u = jnp.sqrt(1.0 - a * a) * (i * ca)
    return a, u


def _branch_a_kernel(xm_ref, xa_ref, ga_ref, cw_ref, cb_ref, wr_ref, br_ref, wi_ref, bi_ref,
                     lam_ref, ya_ref, h_ref, hist_ref, buf, hcar, a_s, u_s, *, tt, kw, n_meta):
    t = pl.program_id(2)
    nt = pl.num_programs(2)
    hb = 8

    def run(x_tile, n):
        buf[hb:hb + n, :] = x_tile
        first = hb - (kw - 1)
        ca = cb_ref[...] + cw_ref[0:1, :] * buf[first:first + n, :]
        for k in range(1, kw):
            ca = ca + cw_ref[k:k + 1, :] * buf[first + k:first + k + n, :]
        tail = buf[hb + n - (kw - 1):hb + n, :]
        buf[first:hb, :] = tail
        a, u = _lru_gates(ca, wr_ref, br_ref, wi_ref, bi_ref, lam_ref)
        grouped = (n // _SCAN_GROUP, _SCAN_GROUP, a.shape[1])
        a = a.reshape(grouped)
        u = u.reshape(grouped)
        in_group = lax.broadcasted_iota(jnp.int32, grouped, 1)
        d = 1
        while d < _SCAN_GROUP:
            keep = in_group >= d
            a_sh = pltpu.roll(a, d, 1)
            u_sh = pltpu.roll(u, d, 1)
            u = jnp.where(keep, a * u_sh + u, u)
            a = jnp.where(keep, a * a_sh, a)
            d *= 2
        a_s[0:n, :] = a.reshape(n, grouped[2])
        u_s[0:n, :] = u.reshape(n, grouped[2])

        def group(g, carry):
            r = pl.multiple_of(g * _SCAN_GROUP, _SCAN_GROUP)
            h_g = u_s[pl.ds(r, _SCAN_GROUP), :] + a_s[pl.ds(r, _SCAN_GROUP), :] * carry
            u_s[pl.ds(r, _SCAN_GROUP), :] = h_g
            return h_g[_SCAN_GROUP - 1:_SCAN_GROUP, :]

        hcar[...] = lax.fori_loop(0, n // _SCAN_GROUP, group, hcar[...])
        return u_s[0:n, :], tail

    @pl.when(t == 0)
    def _():
        buf[0:hb, :] = jnp.zeros((hb, buf.shape[1]), _F32)
        hcar[...] = jnp.zeros_like(hcar)
        run(xm_ref[...], n_meta)

    h, tail = run(xa_ref[...], tt)
    ya_ref[...] = (h * _gelu_tanh(ga_ref[...])).astype(ya_ref.dtype)

    @pl.when(t == nt - 1)
    def _():
        h_ref[...] = h[tt - 1:tt, :]
        hist_ref[...] = tail


def _branch_a_prompt(proj, p, *, n_seq, t_len, d, rows_total, meta_row0, n_meta):
    kw = p["conv_a_w"].shape[0]
    tt = _largest_divisor(t_len, _TT_A_CAP, _SUBLANES_BF16)
    cw = _largest_divisor(d, 512, _HEAD)
    nt, nc = t_len // tt, d // cw
    hpc = cw // _HEAD
    ga_off = d // cw
    mblk = meta_row0 // n_meta
    vec = lambda b, c, t: (0, c)
    kernel = functools.partial(_branch_a_kernel, tt=tt, kw=kw, n_meta=n_meta)
    return pl.pallas_call(
        kernel,
        out_shape=(jax.ShapeDtypeStruct((rows_total, d), _BF16),
                   jax.ShapeDtypeStruct((n_seq, 1, d), _F32),
                   jax.ShapeDtypeStruct((n_seq, kw - 1, d), _F32)),
        grid=(n_seq, nc, nt),
        in_specs=[
            pl.BlockSpec((n_meta, cw), lambda b, c, t: (mblk, c)),
            pl.BlockSpec((tt, cw), lambda b, c, t: (b * nt + t, c)),
            pl.BlockSpec((tt, cw), lambda b, c, t: (b * nt + t, ga_off + c)),
            pl.BlockSpec((kw, cw), vec),
            pl.BlockSpec((1, cw), vec),
            pl.BlockSpec((hpc, _HEAD, _HEAD), lambda b, c, t: (c, 0, 0)),
            pl.BlockSpec((1, cw), vec),
            pl.BlockSpec((hpc, _HEAD, _HEAD), lambda b, c, t: (c, 0, 0)),
            pl.BlockSpec((1, cw), vec),
            pl.BlockSpec((1, cw), vec),
        ],
        out_specs=(
            pl.BlockSpec((tt, cw), lambda b, c, t: (b * nt + t, c)),
            pl.BlockSpec((None, 1, cw), lambda b, c, t: (b, 0, c)),
            pl.BlockSpec((None, kw - 1, cw), lambda b, c, t: (b, 0, c)),
        ),
        scratch_shapes=[pltpu.VMEM((8 + max(tt, n_meta), cw), _F32), pltpu.VMEM((1, cw), _F32),
                        pltpu.VMEM((max(tt, n_meta), cw), _F32),
                        pltpu.VMEM((max(tt, n_meta), cw), _F32)],
        compiler_params=pltpu.CompilerParams(
            dimension_semantics=("arbitrary", "arbitrary", "arbitrary"),
            vmem_limit_bytes=_vmem_limit(24 * tt * cw * 4)),
        name="branch_a_prompt",
    )(proj, proj, proj, p["conv_a_w"], p["conv_a_b"], p["w_r"], p["b_r"], p["w_i"], p["b_i"],
      p["lru_lambda"])


def _layernorm_silu(cb, g, b):
    mu = jnp.mean(cb, axis=-1, keepdims=True)
    xc = cb - mu
    y = xc * lax.rsqrt(jnp.mean(xc * xc, axis=-1, keepdims=True) + _EPS)
    y = y * g + b
    return y * _sigmoid(y)


def _branch_b_kernel(xgm_ref, gtm_ref, xg_ref, gt_ref, cw_ref, cb_ref, lg_ref, lb_ref,
                     yb_ref, hist_ref, buf, cbuf, shifted, *, tt, kw, lane_chunk, n_meta):
    t = pl.program_id(1)
    nt = pl.num_programs(1)
    hb = _round_up(kw - 1, 8)
    d = buf.shape[1]

    @pl.when(t == 0)
    def _():
        buf[0:hb, :] = jnp.zeros((hb, d), _F32)
        buf[hb - n_meta:hb, :] = xgm_ref[...] * _sigmoid(gtm_ref[...])

    buf[hb:hb + tt, :] = xg_ref[...] * _sigmoid(gt_ref[...])
    first = hb - (kw - 1)
    for c0 in range(0, d, lane_chunk):
        cs = slice(c0, c0 + lane_chunk)
        acc = jnp.broadcast_to(cb_ref[:, cs], (tt, lane_chunk))
        for res in range(_SCAN_GROUP):
            taps = [k for k in range(kw) if (first + k) % _SCAN_GROUP == res]
            if not taps:
                continue
            span = _SCAN_GROUP * max((first + k) // _SCAN_GROUP for k in taps) + tt
            shifted[0:span, :] = buf[res:res + span, cs]
            for k in taps:
                q0 = (first + k) // _SCAN_GROUP * _SCAN_GROUP
                acc = acc + cw_ref[k:k + 1, cs] * shifted[q0:q0 + tt, :]
        cbuf[:, cs] = acc
    tail = buf[hb + tt - (kw - 1):hb + tt, :]
    buf[first:hb, :] = tail
    yb_ref[...] = _layernorm_silu(cbuf[...], lg_ref[...], lb_ref[...]).astype(yb_ref.dtype)

    @pl.when(t == nt - 1)
    def _():
        hist_ref[...] = tail


def _branch_b_prompt(proj, p, *, n_seq, t_len, d, rows_total, meta_row0, n_meta):
    kw = p["conv_b_w"].shape[0]
    tt = _largest_divisor(t_len, _TT_B_CAP, _SUBLANES_BF16)
    nt = t_len // tt
    hb = _round_up(kw - 1, 8)
    assert n_meta <= hb and tt >= kw - 1
    mblk = meta_row0 // n_meta
    vec = lambda b, t: (0, 0)
    kernel = functools.partial(_branch_b_kernel, tt=tt, kw=kw, n_meta=n_meta,
                               lane_chunk=_largest_divisor(d, 256, 128))
    return pl.pallas_call(
        kernel,
        out_shape=(jax.ShapeDtypeStruct((rows_total, d), _BF16),
                   jax.ShapeDtypeStruct((n_seq, kw - 1, d), _F32)),
        grid=(n_seq, nt),
        in_specs=[
            pl.BlockSpec((n_meta, d), lambda b, t: (mblk, 2)),
            pl.BlockSpec((n_meta, d), lambda b, t: (mblk, 3)),
            pl.BlockSpec((tt, d), lambda b, t: (b * nt + t, 2)),
            pl.BlockSpec((tt, d), lambda b, t: (b * nt + t, 3)),
            pl.BlockSpec((kw, d), vec),
            pl.BlockSpec((1, d), vec),
            pl.BlockSpec((1, d), vec),
            pl.BlockSpec((1, d), vec),
        ],
        out_specs=(
            pl.BlockSpec((tt, d), lambda b, t: (b * nt + t, 0)),
            pl.BlockSpec((None, kw - 1, d), lambda b, t: (b, 0, 0)),
        ),
        scratch_shapes=[pltpu.VMEM((hb + tt, d), _F32), pltpu.VMEM((tt, d), _F32),
                        pltpu.VMEM((hb + tt, _largest_divisor(d, 256, 128)), _F32)],
        compiler_params=pltpu.CompilerParams(
            dimension_semantics=("arbitrary", "arbitrary"),
            vmem_limit_bytes=_vmem_limit(16 * (hb + tt) * d * 4)),
        name="branch_b_prompt",
    )(proj, proj, proj, proj, p["conv_b_w"], p["conv_b_b"], p["ln_b_g"], p["ln_b_b"])


def _sample_kernel(xa_ref, ga_ref, xg_ref, gt_ref, sh_ref, sa_ref, sb_ref,
                   caw_ref, cab_ref, wr_ref, br_ref, wi_ref, bi_ref, lam_ref,
                   cbw_ref, cbb_ref, lg_ref, lb_ref, ya_in, yb_in,
                   ya_ref, yb_ref, oh_ref, oa_ref, ob_ref, glu_s, ca_s, cb_s,
                   *, tr, n_real, kwa, kwb):
    del ya_in, yb_in
    i = pl.program_id(0)

    @pl.when(i < n_real)
    def _():
        glu_s[...] = xg_ref[...] * _sigmoid(gt_ref[...])

        def row_body(n, carry):
            hb = sb_ref[n]
            g_row = glu_s[pl.ds(n, 1), :]
            cb_s[pl.ds(n, 1), :] = (
                jnp.sum(hb * cbw_ref[0:kwb - 1, :], axis=0, keepdims=True)
                + cbw_ref[kwb - 1:kwb, :] * g_row + cbb_ref[...])
            ob_ref[n, 0:kwb - 2, :] = hb[1:kwb - 1, :]
            ob_ref[n, kwb - 2:kwb - 1, :] = g_row
            ha = sa_ref[n]
            x_row = xa_ref[pl.ds(n, 1), :]
            ca_s[pl.ds(n, 1), :] = (
                jnp.sum(ha * caw_ref[0:kwa - 1, :], axis=0, keepdims=True)
                + caw_ref[kwa - 1:kwa, :] * x_row + cab_ref[...])
            oa_ref[n, 0:kwa - 2, :] = ha[1:kwa - 1, :]
            oa_ref[n, kwa - 2:kwa - 1, :] = x_row
            return carry

        lax.fori_loop(0, tr, row_body, 0)

        ca = ca_s[...]
        a, u = _lru_gates(ca, wr_ref, br_ref, wi_ref, bi_ref, lam_ref)
        h = a * sh_ref[...] + u
        oh_ref[...] = h
        ya_ref[...] = (h * _gelu_tanh(ga_ref[...])).astype(ya_ref.dtype)
        yb_ref[...] = _layernorm_silu(cb_s[...], lg_ref[...], lb_ref[...]).astype(yb_ref.dtype)

    @pl.when(i >= n_real)
    def _():
        ya_ref[...] = jnp.zeros_like(ya_ref)
        yb_ref[...] = jnp.zeros_like(yb_ref)


def _branches_sample(proj, ya, yb, state_h, state_a, state_b, p, *, row0, n_s, d, rows_total):
    tr = _SUBLANES_BF16
    kwa, kwb = p["conv_a_w"].shape[0], p["conv_b_w"].shape[0]
    n_real = n_s // tr
    n_blocks = (rows_total - row0) // tr
    b0 = row0 // tr
    n_heads = d // _HEAD
    full = lambda i: (0, 0)
    st = lambda i: (jnp.minimum(i, n_real - 1), 0)
    st3 = lambda i: (jnp.minimum(i, n_real - 1), 0, 0)
    kernel = functools.partial(_sample_kernel, tr=tr, n_real=n_real, kwa=kwa, kwb=kwb)
    col = lambda c: pl.BlockSpec((tr, d), lambda i, c=c: (b0 + i, c))
    est = 4 * tr * (_round_up(kwb - 1, 8) + 8) * d * 4 + 16 * tr * d * 4 + (kwb + 16) * d * 4
    return pl.pallas_call(
        kernel,
        out_shape=(jax.ShapeDtypeStruct((rows_total, d), _BF16),
                   jax.ShapeDtypeStruct((rows_total, d), _BF16),
                   jax.ShapeDtypeStruct((n_s, d), _F32),
                   jax.ShapeDtypeStruct((n_s, kwa - 1, d), _F32),
                   jax.ShapeDtypeStruct((n_s, kwb - 1, d), _F32)),
        grid=(n_blocks,),
        in_specs=[
            col(0), col(1), col(2), col(3),
            pl.BlockSpec((tr, d), st),
            pl.BlockSpec((tr, kwa - 1, d), st3),
            pl.BlockSpec((tr, kwb - 1, d), st3),
            pl.BlockSpec((kwa, d), full), pl.BlockSpec((1, d), full),
            pl.BlockSpec((n_heads, _HEAD, _HEAD), lambda i: (0, 0, 0)), pl.BlockSpec((1, d), full),
            pl.BlockSpec((n_heads, _HEAD, _HEAD), lambda i: (0, 0, 0)), pl.BlockSpec((1, d), full),
            pl.BlockSpec((1, d), full),
            pl.BlockSpec((kwb, d), full), pl.BlockSpec((1, d), full),
            pl.BlockSpec((1, d), full), pl.BlockSpec((1, d), full),
            pl.BlockSpec(memory_space=pl.ANY), pl.BlockSpec(memory_space=pl.ANY),
        ],
        out_specs=(
            pl.BlockSpec((tr, d), lambda i: (b0 + i, 0)),
            pl.BlockSpec((tr, d), lambda i: (b0 + i, 0)),
            pl.BlockSpec((tr, d), st),
            pl.BlockSpec((tr, kwa - 1, d), st3),
            pl.BlockSpec((tr, kwb - 1, d), st3),
        ),
        scratch_shapes=[pltpu.VMEM((tr, d), _F32)] * 3,
        input_output_aliases={18: 0, 19: 1},
        compiler_params=pltpu.CompilerParams(
            dimension_semantics=("arbitrary",),
            vmem_limit_bytes=_vmem_limit(est)),
        name="branches_sample",
    )(proj, proj, proj, proj, state_h, state_a, state_b,
      p["conv_a_w"], p["conv_a_b"], p["w_r"], p["b_r"], p["w_i"], p["b_i"], p["lru_lambda"],
      p["conv_b_w"], p["conv_b_b"], p["ln_b_g"], p["ln_b_b"], ya, yb)


def _router_kernel(x_ref, g_ref, wr_ref, br_ref, xp_ref, te_ref, tg_ref, rk_ref, cnt_ref,
                   carry, *, n_exp):
    i = pl.program_id(0)

    @pl.when(i == 0)
    def _():
        carry[...] = jnp.zeros_like(carry)

    xn = _rms(x_ref[...], g_ref[...])
    xh = xn.astype(_BF16)
    xhf = xh.astype(_F32)
    half = xn.shape[1] // 2
    bits = lax.bitcast_convert_type(xhf, _U32)
    xp_ref[...] = (bits[:, 0:half] >> jnp.uint32(16)) | bits[:, half:2 * half]

    w = wr_ref[...]
    xl = (xn - xhf).astype(_BF16)
    wh = w.astype(_BF16)
    wl = (w - wh.astype(_F32)).astype(_BF16)
    logits = (jnp.dot(xh, wh, preferred_element_type=_F32)
              + (jnp.dot(xl, wh, preferred_element_type=_F32)
                 + jnp.dot(xh, wl, preferred_element_type=_F32))) + br_ref[...]

    tb = xn.shape[0]
    lane = lax.broadcasted_iota(jnp.int32, (tb, n_exp), 1)
    work = logits
    vals, idxs, hots = [], [], []
    for _ in range(_TOP_K):
        m = jnp.max(work, axis=-1, keepdims=True)
        idx = jnp.min(jnp.where(work == m, lane, n_exp), axis=-1, keepdims=True)
        hot = lane == idx
        vals.append(m); idxs.append(idx); hots.append(hot)
        work = jnp.where(hot, -jnp.inf, work)
    es = [jnp.exp(v - vals[0]) for v in vals]
    den = es[0] + es[1] + es[2] + es[3]
    tg_ref[...] = jnp.concatenate([e / den for e in es], axis=-1)
    te_ref[...] = jnp.concatenate(idxs, axis=-1)

    sel = (hots[0] | hots[1] | hots[2] | hots[3]).astype(_F32)
    r_i = lax.broadcasted_iota(jnp.int32, (tb, tb), 0)
    c_i = lax.broadcasted_iota(jnp.int32, (tb, tb), 1)
    tri = (c_i < r_i).astype(_BF16)
    before = jnp.dot(tri, sel.astype(_BF16), preferred_element_type=_F32) + carry[...]
    ranks = [jnp.sum(jnp.where(h, before, 0.0), axis=-1, keepdims=True) for h in hots]
    rk_ref[...] = jnp.concatenate(ranks, axis=-1).astype(jnp.int32)
    total = carry[...] + jnp.sum(sel, axis=0, keepdims=True)
    carry[...] = total
    cnt_ref[...] = total.astype(jnp.int32)


def _router(x2, g, w_router, b_router):
    rows, d = x2.shape
    n_exp = w_router.shape[1]
    tb = _largest_divisor(rows, 256, 8)
    tok = lambda i: (i, 0)
    full = lambda i: (0, 0)
    return pl.pallas_call(
        functools.partial(_router_kernel, n_exp=n_exp),
        out_shape=(jax.ShapeDtypeStruct((rows, d // 2), _U32),
                   jax.ShapeDtypeStruct((rows, _TOP_K), jnp.int32),
                   jax.ShapeDtypeStruct((rows, _TOP_K), _F32),
                   jax.ShapeDtypeStruct((rows, _TOP_K), jnp.int32),
                   jax.ShapeDtypeStruct((1, n_exp), jnp.int32)),
        grid=(rows // tb,),
        in_specs=[pl.BlockSpec((tb, d), tok), pl.BlockSpec((1, d), full),
                  pl.BlockSpec((d, n_exp), full), pl.BlockSpec((1, n_exp), full)],
        out_specs=(pl.BlockSpec((tb, d // 2), tok), pl.BlockSpec((tb, _TOP_K), tok),
                   pl.BlockSpec((tb, _TOP_K), tok), pl.BlockSpec((tb, _TOP_K), tok),
                   pl.BlockSpec((1, n_exp), full)),
        scratch_shapes=[pltpu.VMEM((1, n_exp), _F32)],
        compiler_params=pltpu.CompilerParams(
            dimension_semantics=("arbitrary",),
            vmem_limit_bytes=_vmem_limit(8 * tb * d * 4)),
        name="router",
    )(x2, g.reshape(1, d), w_router, b_router.reshape(1, n_exp))


def _dispatch_kernel(slot_ref, padrow_ref, padn_ref, x_ref, o_hbm, stage, zrow, sem,
                     *, blk, n_blocks, n_chunks):
    i = pl.program_id(0)
    par = i % 2

    def row_copy(p, src_row, dst_row):
        return pltpu.make_async_copy(stage.at[p, pl.ds(src_row, 1), :],
                                     o_hbm.at[pl.ds(dst_row, 1), :], sem.at[p])

    def drain(p, n):
        def body(t, carry):
            row_copy(p, 0, 0).wait()
            return carry
        lax.fori_loop(0, n, body, 0)

    @pl.when(i >= 2)
    def _():
        drain(par, blk * _TOP_K)

    stage[par] = x_ref[...]

    def issue(t, carry):
        tok = i * blk + t
        for k in range(_TOP_K):
            row_copy(par, t, slot_ref[tok * _TOP_K + k]).start()
        return carry
    lax.fori_loop(0, blk, issue, 0)

    @pl.when(i == n_blocks - 1)
    def _():
        drain(par, blk * _TOP_K)
        if n_blocks > 1:
            drain(1 - par, blk * _TOP_K)
        zrow[...] = jnp.zeros_like(zrow)

        def zero_fill(v, carry):
            n = padn_ref[v]

            def body(r, c2):
                pltpu.make_async_copy(zrow.at[pl.ds(0, 1), :],
                                      o_hbm.at[pl.ds(padrow_ref[v] + r, 1), :], sem.at[0]).start()
                return c2
            lax.fori_loop(0, n, body, 0)
            drain(0, n)
            return carry
        lax.fori_loop(0, n_chunks, zero_fill, 0)


def _dispatch(xp, slots, pad_row, pad_n, *, n_rows_out):
    n_tok, half = xp.shape
    blk = _largest_divisor(n_tok, 256, 8)
    n_blocks = n_tok // blk
    return pl.pallas_call(
        functools.partial(_dispatch_kernel, blk=blk, n_blocks=n_blocks, n_chunks=pad_n.shape[0]),
        out_shape=jax.ShapeDtypeStruct((n_rows_out, half), _U32),
        grid_spec=pltpu.PrefetchScalarGridSpec(
            num_scalar_prefetch=3,
            grid=(n_blocks,),
            in_specs=[pl.BlockSpec((blk, half), lambda i, s, pr, pn: (i, 0))],
            out_specs=pl.BlockSpec(memory_space=pl.ANY),
            scratch_shapes=[pltpu.VMEM((2, blk, half), _U32), pltpu.VMEM((8, half), _U32),
                            pltpu.SemaphoreType.DMA((2,))]),
        compiler_params=pltpu.CompilerParams(
            dimension_semantics=("arbitrary",),
            vmem_limit_bytes=_vmem_limit(6 * blk * half * 4)),
        name="dispatch",
    )(slots, pad_row, pad_n, xp)


def _combine_kernel(slot_ref, ys_hbm, g_ref, x_ref, gf_ref, om_ref, ot_ref, buf, sem,
                    *, tb, n_main_blocks):
    i = pl.program_id(0)
    n = pl.num_programs(0)

    def copy(b, r, k, slot_row):
        par = b % 2
        return pltpu.make_async_copy(ys_hbm.at[pl.ds(slot_row, 1), :],
                                     buf.at[par, k, pl.ds(r, 1), :], sem.at[par])

    def issue(b):
        base = b * (tb * _TOP_K)

        def body(r, carry):
            for k in range(_TOP_K):
                copy(b, r, k, slot_ref[base + r * _TOP_K + k]).start()
            return carry
        lax.fori_loop(0, tb, body, 0)

    @pl.when(i == 0)
    def _():
        issue(i)

    @pl.when(i + 1 < n)
    def _():
        issue(i + 1)

    def drain(r, carry):
        for k in range(_TOP_K):
            copy(i, r, k, 0).wait()
        return carry
    lax.fori_loop(0, tb, drain, 0)

    par = i % 2
    g = g_ref[...]
    moe = g[:, 0:1] * buf[par, 0]
    for k in range(1, _TOP_K):
        moe = moe + g[:, k:k + 1] * buf[par, k]
    y = _rms(x_ref[...] + moe, gf_ref[...])

    @pl.when(i < n_main_blocks)
    def _():
        om_ref[...] = y

    @pl.when(i >= n_main_blocks)
    def _():
        ot_ref[...] = y


def _combine(ys, slots, gates, x2, g_final, *, n_main, tb):
    rows, d = x2.shape
    nmb = n_main // tb
    tok = lambda i, s: (i, 0)
    return pl.pallas_call(
        functools.partial(_combine_kernel, tb=tb, n_main_blocks=nmb),
        out_shape=(jax.ShapeDtypeStruct((n_main, d), _F32),
                   jax.ShapeDtypeStruct((rows - n_main, d), _F32)),
        grid_spec=pltpu.PrefetchScalarGridSpec(
            num_scalar_prefetch=1,
            grid=(rows // tb,),
            in_specs=[pl.BlockSpec(memory_space=pl.ANY),
                      pl.BlockSpec((tb, _TOP_K), tok),
                      pl.BlockSpec((tb, d), tok),
                      pl.BlockSpec((1, d), lambda i, s: (0, 0))],
            out_specs=(pl.BlockSpec((tb, d), lambda i, s: (jnp.minimum(i, nmb - 1), 0)),
                       pl.BlockSpec((tb, d), lambda i, s: (jnp.maximum(i - nmb, 0), 0))),
            scratch_shapes=[pltpu.VMEM((2, _TOP_K, tb, d), _F32),
                            pltpu.SemaphoreType.DMA((2,))]),
        compiler_params=pltpu.CompilerParams(
            dimension_semantics=("arbitrary",),
            vmem_limit_bytes=_vmem_limit((2 * _TOP_K + 10) * tb * d * 4)),
        name="combine",
    )(slots, ys, gates, x2, g_final.reshape(1, d))


def _moe_tables(counts, top_e, rank, *, tm, sub, quantum, n_chunks):
    n_exp = counts.shape[0]
    chunks_e = (counts + tm - 1) // tm
    chunk_end = jnp.cumsum(chunks_e)
    chunk_start = chunk_end - chunks_e
    n_used = chunk_end[-1]
    v = jnp.arange(n_chunks, dtype=jnp.int32)
    v_eff = jnp.minimum(v, n_used - 1)
    e_v = jnp.minimum(jnp.searchsorted(chunk_end, v_eff, side="right"), n_exp - 1).astype(jnp.int32)
    within = v_eff - chunk_start[e_v]
    rows_v = jnp.where(v < n_used, jnp.clip(counts[e_v] - within * tm, 0, tm), 0).astype(jnp.int32)
    pad_row = (v_eff * tm + rows_v).astype(jnp.int32)
    filled = jnp.maximum((rows_v + quantum - 1) // quantum * quantum, sub)
    pad_n = jnp.where(rows_v > 0, filled - rows_v, 0).astype(jnp.int32)
    slot = (chunk_start[top_e] * tm + rank).astype(jnp.int32)
    return (v_eff.astype(jnp.int32), e_v, rows_v), pad_row, pad_n, slot


def kernel(x_prompt, x_sample, state_lru_h, state_lru_conv, state_conf_conv, meta_tokens,
           norm_mix_g, w_in, conv_a_w, conv_a_b, w_r, b_r, w_i, b_i, lru_lambda,
           conv_b_w, conv_b_b, ln_b_g, ln_b_b, w_a_out, w_b_out, w_o, norm_ffn_g,
           w_router, b_router, w_gate, b_gate, w_up, b_up, w_down, b_down, norm_final_g):
    assert w_in.shape[0] == 1, "one layer"
    n_b, seq, d = x_prompt.shape
    n_s = x_sample.shape[0]
    n_meta = meta_tokens.shape[0]
    n_main = n_b * seq
    n_exp = w_router.shape[2]
    assert x_sample.shape[1] == 1 and n_s % _SUBLANES_BF16 == 0 and seq % _SUBLANES_BF16 == 0
    assert d % _HEAD == 0 and w_r.shape[2] == _HEAD
    assert n_meta % _SUBLANES_BF16 == 0 and (n_main + n_s) % n_meta == 0

    tr = _largest_divisor(n_main, 256, _SUBLANES_BF16)
    rows_tail = _round_up(n_s + n_meta, tr)
    rows = n_main + rows_tail
    meta_row0 = n_main + n_s
    tm_d = _largest_divisor(rows, 2112, 32)
    tm_o = _largest_divisor(rows, 1056, 32)
    sub_d = _largest_divisor(tm_d, 704, _SUBLANES_BF16)
    sub_o = _largest_divisor(tm_o, 704, _SUBLANES_BF16)
    tm_2 = _largest_divisor(rows, 1408, 32)
    sub_2 = _largest_divisor(tm_2, 704, _SUBLANES_BF16)
    tn = _largest_divisor(d, 512, 128)
    tn_2 = _largest_divisor(d, 256, 128)
    big = rows >= 4096
    tm_e = _EXPERT_CHUNK if big else 128
    sub_e = _EXPERT_SUB if big else 64
    q_e = _EXPERT_QUANTUM if big else 16
    n_pairs = rows * _TOP_K
    n_chunks = n_exp + n_pairs // tm_e

    x_tail = jnp.concatenate([x_sample.reshape(n_s, d), meta_tokens.astype(x_prompt.dtype),
                              jnp.zeros((rows_tail - n_s - n_meta, d), x_prompt.dtype)], axis=0)

    p = dict(conv_a_w=conv_a_w[0], conv_a_b=conv_a_b[0].reshape(1, d),
             w_r=w_r[0], b_r=b_r[0].reshape(1, d), w_i=w_i[0], b_i=b_i[0].reshape(1, d),
             lru_lambda=lru_lambda[0].reshape(1, d),
             conv_b_w=conv_b_w[0], conv_b_b=conv_b_b[0].reshape(1, d),
             ln_b_g=ln_b_g[0].reshape(1, d), ln_b_b=ln_b_b[0].reshape(1, d))

    x, xn = _rmsnorm_rows(x_prompt.reshape(n_main, d), x_tail, norm_mix_g[0], tr)
    proj = _gmm([xn], [w_in], [(0, 0)], _dense_tables(rows, tm_d),
                tm=tm_d, tn=tn, sub=sub_d, out_dtype=_F32, single_buffer_lhs=True,
                epilogue=lambda pr, ex: pr[0], name="in_proj")
    seq_args = dict(n_seq=n_b, t_len=seq, d=d, rows_total=rows, meta_row0=meta_row0, n_meta=n_meta)
    ya, h_p, ha_p = _branch_a_prompt(proj, p, **seq_args)
    yb, hb_p = _branch_b_prompt(proj, p, **seq_args)
    ya, yb, h_s, ha_s, hb_s = _branches_sample(
        proj, ya, yb, state_lru_h[0], state_lru_conv[0], state_conf_conv[0], p,
        row0=n_main, n_s=n_s, d=d, rows_total=rows)
    gate_off = 4 * d // tn_2
    merged = _gmm([ya, yb], [w_a_out, w_b_out], [(0, 0), (1, 1)],
                  _dense_tables(rows, tm_2), tm=tm_2, tn=tn_2, sub=sub_2,
                  out_dtype=_BF16, name="merge_proj", single_buffer_lhs=True,
                  extras=[(proj, gate_off), (proj, gate_off + d // tn_2)],
                  epilogue=lambda pr, ex: _sigmoid(ex[0]) * pr[0] + _sigmoid(ex[1]) * pr[1])
    x2 = _gmm([merged], [w_o], [(0, 0)], _dense_tables(rows, tm_o),
              tm=tm_o, tn=tn, sub=sub_o, out_dtype=_F32, name="out_proj",
              extras=[(x, 0)], epilogue=lambda pr, ex: ex[0] + pr[0])

    xp, top_e, gates, rank, counts = _router(x2, norm_ffn_g[0], w_router[0], b_router[0])
    tables, pad_row, pad_n, slot = _moe_tables(
        counts[0], top_e, rank, tm=tm_e, sub=sub_e, quantum=q_e, n_chunks=n_chunks)
    slots = slot.reshape(-1)
    xs = _dispatch(xp, slots, pad_row, pad_n, n_rows_out=n_chunks * tm_e)

    def swiglu(pr, ex):
        g = jnp.minimum(pr[0], _SWIGLU_LIMIT)
        u = jnp.clip(pr[1], -_SWIGLU_LIMIT, _SWIGLU_LIMIT)
        return g * _sigmoid(_SWIGLU_ALPHA * g) * (u + 1.0)

    hid = _gmm([xs], [w_gate[0], w_up[0]], [(0, 0), (0, 1)], tables, tm=tm_e, tn=tn_2,
               sub=sub_e, quantum=q_e, out_dtype=_BF16, name="expert_up", epilogue=swiglu,
               packed=True,
               biases=[b_gate[0][:, None, :], b_up[0][:, None, :]])
    ys = _gmm([hid], [w_down[0]], [(0, 0)], tables, tm=tm_e, tn=tn, sub=sub_e, quantum=q_e,
              out_dtype=_F32, name="expert_down", epilogue=lambda pr, ex: pr[0],
              biases=[b_down[0][:, None, :]])
    tb_c = _largest_divisor(math.gcd(n_main, rows_tail), 64, 8)
    y_main, y_tail = _combine(ys, slots, gates, x2, norm_final_g, n_main=n_main, tb=tb_c)

    return (y_main.reshape(n_b, seq, d), y_tail[:n_s].reshape(n_s, 1, d),
            h_p.reshape(1, n_b, d), ha_p[None], hb_p[None], h_s[None], ha_s[None], hb_s[None])
```

```python
import functools
import math

import jax
import jax.numpy as jnp
from jax import lax
from jax.experimental import pallas as pl
from jax.experimental.pallas import tpu as pltpu

_F32 = jnp.float32
_BF16 = jnp.bfloat16
_U32 = jnp.uint32

_EPS = 1e-6
_LRU_C = 8.0
_SWIGLU_LIMIT = 7.0
_SWIGLU_ALPHA = 1.702
_TOP_K = 4
_HEAD = 256

_V7X_VMEM_BYTES = 64 * 1024 * 1024
_VMEM_CAP = _V7X_VMEM_BYTES - 5 * 1024 * 1024
_SUBLANES_BF16 = 16
_SCAN_GROUP = 8

_TT_A_CAP = 512
_TT_B_CAP = 64
_EXPERT_CHUNK = 1280
_EXPERT_SUB = 256
_EXPERT_QUANTUM = 64


def _vmem_limit(estimate_bytes):
    return int(min(max(estimate_bytes + (6 << 20), 24 << 20), _VMEM_CAP))


def _largest_divisor(n, cap, mult):
    best = None
    for d in range(mult, min(n, cap) + 1, mult):
        if n % d == 0:
            best = d
    assert best is not None, (n, cap, mult)
    return best


def _round_up(n, m):
    return (n + m - 1) // m * m


def _sigmoid(x):
    return jax.nn.sigmoid(x)


def _gelu_tanh(x):
    c = math.sqrt(2.0 / math.pi)
    return 0.5 * x * (1.0 + jnp.tanh(c * (x + 0.044715 * (x * x * x))))


def _softplus(x):
    return jnp.maximum(x, 0.0) + jnp.log1p(jnp.exp(-jnp.abs(x)))


def _rms(x, g):
    ms = jnp.mean(x * x, axis=-1, keepdims=True)
    return (x * lax.rsqrt(ms + _EPS)) * g


def _rmsnorm_kernel(xm_ref, xt_ref, g_ref, x_ref, o_ref, *, n_main_blocks):
    i = pl.program_id(0)

    def emit(x):
        x_ref[...] = x
        o_ref[...] = _rms(x, g_ref[...]).astype(o_ref.dtype)

    @pl.when(i < n_main_blocks)
    def _():
        emit(xm_ref[...])

    @pl.when(i >= n_main_blocks)
    def _():
        emit(xt_ref[...])


def _rmsnorm_rows(x_main, x_tail, g, tr):
    n_main, d = x_main.shape
    rows = n_main + x_tail.shape[0]
    nmb = n_main // tr
    return pl.pallas_call(
        functools.partial(_rmsnorm_kernel, n_main_blocks=nmb),
        out_shape=(jax.ShapeDtypeStruct((rows, d), _F32),
                   jax.ShapeDtypeStruct((rows, d), _BF16)),
        grid=(rows // tr,),
        in_specs=[pl.BlockSpec((tr, d), lambda i: (jnp.minimum(i, nmb - 1), 0)),
                  pl.BlockSpec((tr, d), lambda i: (jnp.maximum(i - nmb, 0), 0)),
                  pl.BlockSpec((1, d), lambda i: (0, 0))],
        out_specs=(pl.BlockSpec((tr, d), lambda i: (i, 0)),
                   pl.BlockSpec((tr, d), lambda i: (i, 0))),
        compiler_params=pltpu.CompilerParams(
            dimension_semantics=("arbitrary",),
            vmem_limit_bytes=_vmem_limit(10 * tr * d * 4)),
        name="rmsnorm_rows",
    )(x_main, x_tail, g.reshape(1, d))


def _gmm_kernel(blk_ref, eid_ref, nrow_ref, *refs, n_lhs, n_w, has_bias, n_extra, pairs,
                tm, tn, sub, quantum, cast_rows, epilogue, packed, fused):
    del blk_ref, eid_ref
    pos = 0
    lhs_refs = refs[pos:pos + n_lhs]; pos += n_lhs
    w_refs = refs[pos:pos + n_w]; pos += n_w
    if has_bias:
        b_refs = refs[pos:pos + n_w]; pos += n_w
    ex_refs = refs[pos:pos + n_extra]; pos += n_extra
    o_ref = refs[pos]; pos += 1
    n_wbf = 1 if fused else n_w
    wbf_refs = refs[pos:pos + n_wbf]; pos += n_wbf
    if packed:
        lhs_refs = (refs[pos],)
        packed_ref = refs[0]

    nrows = nrow_ref[pl.program_id(0)]

    @pl.when(nrows > 0)
    def _():
        k_dim = w_refs[0].shape[0]

        def cast_body(c, carry):
            r = pl.multiple_of(c * cast_rows, cast_rows)
            for wi, w_ref in enumerate(w_refs):
                tile = w_ref[pl.ds(r, cast_rows), :].astype(_BF16)
                if fused:
                    wbf_refs[0][pl.ds(r, cast_rows), wi * tn:(wi + 1) * tn] = tile
                else:
                    wbf_refs[wi][pl.ds(r, cast_rows), :] = tile
            return carry

        lax.fori_loop(0, k_dim // cast_rows, cast_body, 0)

        if quantum is not None:
            per = sub // quantum
            n_q = jnp.maximum((nrows + quantum - 1) // quantum, per)

        if packed:
            half = k_dim // 2

            @pl.when(pl.program_id(1) == 0)
            def _():
                def unpack_body(q, carry):
                    r = pl.multiple_of(q * quantum, quantum)
                    w = packed_ref[pl.ds(r, quantum), :]
                    lo = lax.bitcast_convert_type(w << jnp.uint32(16), _F32)
                    hi = lax.bitcast_convert_type(w & jnp.uint32(0xFFFF0000), _F32)
                    lhs_refs[0][pl.ds(r, quantum), 0:half] = lo.astype(_BF16)
                    lhs_refs[0][pl.ds(r, quantum), half:k_dim] = hi.astype(_BF16)
                    return carry

                lax.fori_loop(0, n_q, unpack_body, 0)

        def compute(r, n):
            if fused:
                p = jnp.dot(lhs_refs[0][pl.ds(r, n), :], wbf_refs[0][...],
                            preferred_element_type=_F32)
                prods = [p[:, wi * tn:(wi + 1) * tn] for wi in range(n_w)]
            else:
                prods = [jnp.dot(lhs_refs[li][pl.ds(r, n), :], wbf_refs[wi][...],
                                 preferred_element_type=_F32) for (li, wi) in pairs]
            if has_bias:
                prods = [p + b_refs[wi][...] for p, (_, wi) in zip(prods, pairs)]
            extras = [e[pl.ds(r, n), :] for e in ex_refs]
            o_ref[pl.ds(r, n), :] = epilogue(prods, extras).astype(o_ref.dtype)

        def sub_body(s, carry):
            compute(pl.multiple_of(s * sub, sub), sub)
            return carry

        if quantum is None:
            lax.fori_loop(0, tm // sub, sub_body, 0)
        else:
            n_full = n_q // per
            n_tail = n_q - n_full * per
            lax.fori_loop(0, jnp.where(n_tail > 0, n_full - 1, n_full), sub_body, 0)
            for tq in range(1, per):
                @pl.when(n_tail == tq)
                def _(tq=tq):
                    compute(pl.multiple_of((n_full - 1) * sub, sub), sub + tq * quantum)


def _gmm(lhs, ws, pairs, tables, *, tm, tn, sub, out_dtype, epilogue, name, quantum=None,
         biases=None, extras=(), packed=False, single_buffer_lhs=False):
    blk, eid, nrow = tables
    n_chunks = blk.shape[0]
    rows = lhs[0].shape[0]
    k_dim = ws[0].shape[1]
    n_out = ws[0].shape[2]
    nj = n_out // tn
    fused = len(lhs) == 1 and len(ws) > 1
    assert n_out % tn == 0 and tm % sub == 0 and rows % tm == 0
    assert quantum is None or sub % quantum == 0
    cast_rows = _largest_divisor(k_dim, 512, _SUBLANES_BF16)

    def jj(v, j, nrow_ref):
        return jnp.where(nrow_ref[v] > 0, j, nj - 1)

    lhs_mode = dict(pipeline_mode=pl.Buffered(1)) if single_buffer_lhs else {}
    in_specs = []
    for a in lhs:
        in_specs.append(pl.BlockSpec((tm, a.shape[1]), lambda v, j, b, e, n: (b[v], 0), **lhs_mode))
    for _ in ws:
        in_specs.append(pl.BlockSpec((None, k_dim, tn),
                                     lambda v, j, b, e, n: (e[v], 0, jj(v, j, n))))
    has_bias = biases is not None
    if has_bias:
        for _ in ws:
            in_specs.append(pl.BlockSpec((None, 1, tn),
                                         lambda v, j, b, e, n: (e[v], 0, jj(v, j, n))))
    for (_, off) in extras:
        in_specs.append(pl.BlockSpec(
            (tm, tn), lambda v, j, b, e, n, off=off: (b[v], off + jj(v, j, n))))
    out_spec = pl.BlockSpec((tm, tn), lambda v, j, b, e, n: (b[v], jj(v, j, n)))

    scratch = ([pltpu.VMEM((k_dim, len(ws) * tn), _BF16)] if fused
               else [pltpu.VMEM((k_dim, tn), _BF16) for _ in ws])
    if packed:
        scratch.append(pltpu.VMEM((tm, k_dim), _BF16))
    out_bytes = jnp.dtype(out_dtype).itemsize
    lhs_bufs = 1 if single_buffer_lhs else 2
    est = (lhs_bufs * len(lhs) * tm * k_dim * 2 + (tm * k_dim * 2 if packed else 0)
           + len(ws) * (2 * k_dim * tn * 4 + k_dim * tn * 2)
           + 2 * len(extras) * tm * tn * 4 + 2 * tm * tn * out_bytes
           + (3 + len(pairs)) * sub * tn * 4)
    kernel = functools.partial(
        _gmm_kernel, n_lhs=len(lhs), n_w=len(ws), has_bias=has_bias, n_extra=len(extras),
        pairs=tuple(pairs), tm=tm, tn=tn, sub=sub, quantum=quantum, cast_rows=cast_rows,
        epilogue=epilogue, packed=packed, fused=fused)
    args = list(lhs) + list(ws) + (list(biases) if has_bias else []) + [a for (a, _) in extras]
    return pl.pallas_call(
        kernel,
        out_shape=jax.ShapeDtypeStruct((rows, n_out), out_dtype),
        grid_spec=pltpu.PrefetchScalarGridSpec(
            num_scalar_prefetch=3,
            grid=(n_chunks, nj),
            in_specs=in_specs,
            out_specs=out_spec,
            scratch_shapes=scratch),
        compiler_params=pltpu.CompilerParams(
            dimension_semantics=("arbitrary", "arbitrary"),
            vmem_limit_bytes=_vmem_limit(est)),
        name=name,
    )(blk, eid, nrow, *args)


def _dense_tables(rows, tm):
    n = rows // tm
    return (jnp.arange(n, dtype=jnp.int32), jnp.zeros((n,), jnp.int32),
            jnp.full((n,), tm, jnp.int32))


def _lru_gates(ca, wr_ref, br_ref, wi_ref, bi_ref, lam_ref):
    cab = ca.astype(_BF16)
    n_heads = wr_ref.shape[0]
    zr, zi = [], []
    for hh in range(n_heads):
        c_h = cab[:, hh * _HEAD:(hh + 1) * _HEAD]
        zr.append(jnp.dot(c_h, wr_ref[hh].astype(_BF16), preferred_element_type=_F32))
        zi.append(jnp.dot(c_h, wi_ref[hh].astype(_BF16), preferred_element_type=_F32))
    zr = jnp.concatenate(zr, axis=-1) if n_heads > 1 else zr[0]
    zi = jnp.concatenate(zi, axis=-1) if n_heads > 1 else zi[0]
    r = _sigmoid(zr + br_ref[...])
    i = _sigmoid(zi + bi_ref[...])
    log_a = (-_LRU_C * r) * _softplus(-lam_ref[...])
    a = jnp.exp(log_a)
    u = jnp.sqrt(1.0 - a * a) * (i * ca)
    return a, u


def _branch_a_kernel(xm_ref, xa_ref, ga_ref, cw_ref, cb_ref, wr_ref, br_ref, wi_ref, bi_ref,
                     lam_ref, ya_ref, h_ref, hist_ref, buf, hcar, a_s, u_s, *, tt, kw, n_meta):
    t = pl.program_id(2)
    nt = pl.num_programs(2)
    hb = 8

    def run(x_tile, n):
        buf[hb:hb + n, :] = x_tile
        first = hb - (kw - 1)
        ca = cb_ref[...] + cw_ref[0:1, :] * buf[first:first + n, :]
        for k in range(1, kw):
            ca = ca + cw_ref[k:k + 1, :] * buf[first + k:first + k + n, :]
        tail = buf[hb + n - (kw - 1):hb + n, :]
        buf[first:hb, :] = tail
        a, u = _lru_gates(ca, wr_ref, br_ref, wi_ref, bi_ref, lam_ref)
        grouped = (n // _SCAN_GROUP, _SCAN_GROUP, a.shape[1])
        a = a.reshape(grouped)
        u = u.reshape(grouped)
        in_group = lax.broadcasted_iota(jnp.int32, grouped, 1)
        d = 1
        while d < _SCAN_GROUP:
            keep = in_group >= d
            a_sh = pltpu.roll(a, d, 1)
            u_sh = pltpu.roll(u, d, 1)
            u = jnp.where(keep, a * u_sh + u, u)
            a = jnp.where(keep, a * a_sh, a)
            d *= 2
        a_s[0:n, :] = a.reshape(n, grouped[2])
        u_s[0:n, :] = u.reshape(n, grouped[2])

        def group(g, carry):
            r = pl.multiple_of(g * _SCAN_GROUP, _SCAN_GROUP)
            h_g = u_s[pl.ds(r, _SCAN_GROUP), :] + a_s[pl.ds(r, _SCAN_GROUP), :] * carry
            u_s[pl.ds(r, _SCAN_GROUP), :] = h_g
            return h_g[_SCAN_GROUP - 1:_SCAN_GROUP, :]

        hcar[...] = lax.fori_loop(0, n // _SCAN_GROUP, group, hcar[...])
        return u_s[0:n, :], tail

    @pl.when(t == 0)
    def _():
        buf[0:hb, :] = jnp.zeros((hb, buf.shape[1]), _F32)
        hcar[...] = jnp.zeros_like(hcar)
        run(xm_ref[...], n_meta)

    h, tail = run(xa_ref[...], tt)
    ya_ref[...] = (h * _gelu_tanh(ga_ref[...])).astype(ya_ref.dtype)

    @pl.when(t == nt - 1)
    def _():
        h_ref[...] = h[tt - 1:tt, :]
        hist_ref[...] = tail


def _branch_a_prompt(proj, p, *, n_seq, t_len, d, rows_total, meta_row0, n_meta):
    kw = p["conv_a_w"].shape[0]
    tt = _largest_divisor(t_len, _TT_A_CAP, _SUBLANES_BF16)
    cw = _largest_divisor(d, 512, _HEAD)
    nt, nc = t_len // tt, d // cw
    hpc = cw // _HEAD
    ga_off = d // cw
    mblk = meta_row0 // n_meta
    vec = lambda b, c, t: (0, c)
    kernel = functools.partial(_branch_a_kernel, tt=tt, kw=kw, n_meta=n_meta)
    return pl.pallas_call(
        kernel,
        out_shape=(jax.ShapeDtypeStruct((rows_total, d), _BF16),
                   jax.ShapeDtypeStruct((n_seq, 1, d), _F32),
                   jax.ShapeDtypeStruct((n_seq, kw - 1, d), _F32)),
        grid=(n_seq, nc, nt),
        in_specs=[
            pl.BlockSpec((n_meta, cw), lambda b, c, t: (mblk, c)),
            pl.BlockSpec((tt, cw), lambda b, c, t: (b * nt + t, c)),
            pl.BlockSpec((tt, cw), lambda b, c, t: (b * nt + t, ga_off + c)),
            pl.BlockSpec((kw, cw), vec),
            pl.BlockSpec((1, cw), vec),
            pl.BlockSpec((hpc, _HEAD, _HEAD), lambda b, c, t: (c, 0, 0)),
            pl.BlockSpec((1, cw), vec),
            pl.BlockSpec((hpc, _HEAD, _HEAD), lambda b, c, t: (c, 0, 0)),
            pl.BlockSpec((1, cw), vec),
            pl.BlockSpec((1, cw), vec),
        ],
        out_specs=(
            pl.BlockSpec((tt, cw), lambda b, c, t: (b * nt + t, c)),
            pl.BlockSpec((None, 1, cw), lambda b, c, t: (b, 0, c)),
            pl.BlockSpec((None, kw - 1, cw), lambda b, c, t: (b, 0, c)),
        ),
        scratch_shapes=[pltpu.VMEM((8 + max(tt, n_meta), cw), _F32), pltpu.VMEM((1, cw), _F32),
                        pltpu.VMEM((max(tt, n_meta), cw), _F32),
                        pltpu.VMEM((max(tt, n_meta), cw), _F32)],
        compiler_params=pltpu.CompilerParams(
            dimension_semantics=("arbitrary", "arbitrary", "arbitrary"),
            vmem_limit_bytes=_vmem_limit(24 * tt * cw * 4)),
        name="branch_a_prompt",
    )(proj, proj, proj, p["conv_a_w"], p["conv_a_b"], p["w_r"], p["b_r"], p["w_i"], p["b_i"],
      p["lru_lambda"])


def _layernorm_silu(cb, g, b):
    mu = jnp.mean(cb, axis=-1, keepdims=True)
    xc = cb - mu
    y = xc * lax.rsqrt(jnp.mean(xc * xc, axis=-1, keepdims=True) + _EPS)
    y = y * g + b
    return y * _sigmoid(y)


def _branch_b_kernel(xgm_ref, gtm_ref, xg_ref, gt_ref, cw_ref, cb_ref, lg_ref, lb_ref,
                     yb_ref, hist_ref, buf, cbuf, shifted, *, tt, kw, lane_chunk, n_meta):
    t = pl.program_id(1)
    nt = pl.num_programs(1)
    hb = _round_up(kw - 1, 8)
    d = buf.shape[1]

    @pl.when(t == 0)
    def _():
        buf[0:hb, :] = jnp.zeros((hb, d), _F32)
        buf[hb - n_meta:hb, :] = xgm_ref[...] * _sigmoid(gtm_ref[...])

    buf[hb:hb + tt, :] = xg_ref[...] * _sigmoid(gt_ref[...])
    first = hb - (kw - 1)
    for c0 in range(0, d, lane_chunk):
        cs = slice(c0, c0 + lane_chunk)
        acc = jnp.broadcast_to(cb_ref[:, cs], (tt, lane_chunk))
        for res in range(_SCAN_GROUP):
            taps = [k for k in range(kw) if (first + k) % _SCAN_GROUP == res]
            if not taps:
                continue
            span = _SCAN_GROUP * max((first + k) // _SCAN_GROUP for k in taps) + tt
            shifted[0:span, :] = buf[res:res + span, cs]
            for k in taps:
                q0 = (first + k) // _SCAN_GROUP * _SCAN_GROUP
                acc = acc + cw_ref[k:k + 1, cs] * shifted[q0:q0 + tt, :]
        cbuf[:, cs] = acc
    tail = buf[hb + tt - (kw - 1):hb + tt, :]
    buf[first:hb, :] = tail
    yb_ref[...] = _layernorm_silu(cbuf[...], lg_ref[...], lb_ref[...]).astype(yb_ref.dtype)

    @pl.when(t == nt - 1)
    def _():
        hist_ref[...] = tail


def _branch_b_prompt(proj, p, *, n_seq, t_len, d, rows_total, meta_row0, n_meta):
    kw = p["conv_b_w"].shape[0]
    tt = _largest_divisor(t_len, _TT_B_CAP, _SUBLANES_BF16)
    nt = t_len // tt
    hb = _round_up(kw - 1, 8)
    assert n_meta <= hb and tt >= kw - 1
    mblk = meta_row0 // n_meta
    vec = lambda b, t: (0, 0)
    kernel = functools.partial(_branch_b_kernel, tt=tt, kw=kw, n_meta=n_meta,
                               lane_chunk=_largest_divisor(d, 256, 128))
    return pl.pallas_call(
        kernel,
        out_shape=(jax.ShapeDtypeStruct((rows_total, d), _BF16),
                   jax.ShapeDtypeStruct((n_seq, kw - 1, d), _F32)),
        grid=(n_seq, nt),
        in_specs=[
            pl.BlockSpec((n_meta, d), lambda b, t: (mblk, 2)),
            pl.BlockSpec((n_meta, d), lambda b, t: (mblk, 3)),
            pl.BlockSpec((tt, d), lambda b, t: (b * nt + t, 2)),
            pl.BlockSpec((tt, d), lambda b, t: (b * nt + t, 3)),
            pl.BlockSpec((kw, d), vec),
            pl.BlockSpec((1, d), vec),
            pl.BlockSpec((1, d), vec),
            pl.BlockSpec((1, d), vec),
        ],
        out_specs=(
            pl.BlockSpec((tt, d), lambda b, t: (b * nt + t, 0)),
            pl.BlockSpec((None, kw - 1, d), lambda b, t: (b, 0, 0)),
        ),
        scratch_shapes=[pltpu.VMEM((hb + tt, d), _F32), pltpu.VMEM((tt, d), _F32),
                        pltpu.VMEM((hb + tt, _largest_divisor(d, 256, 128)), _F32)],
        compiler_params=pltpu.CompilerParams(
            dimension_semantics=("arbitrary", "arbitrary"),
            vmem_limit_bytes=_vmem_limit(16 * (hb + tt) * d * 4)),
        name="branch_b_prompt",
    )(proj, proj, proj, proj, p["conv_b_w"], p["conv_b_b"], p["ln_b_g"], p["ln_b_b"])


def _sample_kernel(xa_ref, ga_ref, xg_ref, gt_ref, sh_ref, sa_ref, sb_ref,
                   caw_ref, cab_ref, wr_ref, br_ref, wi_ref, bi_ref, lam_ref,
                   cbw_ref, cbb_ref, lg_ref, lb_ref, ya_in, yb_in,
                   ya_ref, yb_ref, oh_ref, oa_ref, ob_ref, *, n_real, kwa, kwb):
    del ya_in, yb_in
    i = pl.program_id(0)

    def conv_step(hist_ref, new_ref, x_new, w_ref, b_ref, kw):
        acc = b_ref[...] + w_ref[kw - 1:kw, :] * x_new
        for k in range(kw - 1):
            h_k = hist_ref[k]
            acc = acc + w_ref[k:k + 1, :] * h_k
            if k >= 1:
                new_ref[k - 1] = h_k
        new_ref[kw - 2] = x_new
        return acc

    @pl.when(i < n_real)
    def _():
        glu = xg_ref[...] * _sigmoid(gt_ref[...])
        cb = conv_step(sb_ref, ob_ref, glu, cbw_ref, cbb_ref, kwb)
        ca = conv_step(sa_ref, oa_ref, xa_ref[...], caw_ref, cab_ref, kwa)
        a, u = _lru_gates(ca, wr_ref, br_ref, wi_ref, bi_ref, lam_ref)
        h = a * sh_ref[...] + u
        oh_ref[...] = h
        ya_ref[...] = (h * _gelu_tanh(ga_ref[...])).astype(ya_ref.dtype)
        yb_ref[...] = _layernorm_silu(cb, lg_ref[...], lb_ref[...]).astype(yb_ref.dtype)

    @pl.when(i >= n_real)
    def _():
        ya_ref[...] = jnp.zeros_like(ya_ref)
        yb_ref[...] = jnp.zeros_like(yb_ref)


def _branches_sample(proj, ya, yb, state_h, state_a, state_b, p, *, row0, n_s, d, rows_total):
    tr = _SUBLANES_BF16
    kwa, kwb = p["conv_a_w"].shape[0], p["conv_b_w"].shape[0]
    n_real = n_s // tr
    n_blocks = (rows_total - row0) // tr
    b0 = row0 // tr
    n_heads = d // _HEAD
    full = lambda i: (0, 0)
    st = lambda i: (jnp.minimum(i, n_real - 1), 0)
    st3 = lambda i: (0, jnp.minimum(i, n_real - 1), 0)
    kernel = functools.partial(_sample_kernel, n_real=n_real, kwa=kwa, kwb=kwb)
    col = lambda c: pl.BlockSpec((tr, d), lambda i, c=c: (b0 + i, c))
    est = (4 * tr * (kwa + kwb) * d * 4 + 2 * n_heads * _HEAD * _HEAD * 4
           + 24 * tr * d * 4 + 2 * (kwb + 16) * d * 4)
    return pl.pallas_call(
        kernel,
        out_shape=(jax.ShapeDtypeStruct((rows_total, d), _BF16),
                   jax.ShapeDtypeStruct((rows_total, d), _BF16),
                   jax.ShapeDtypeStruct((n_s, d), _F32),
                   jax.ShapeDtypeStruct((kwa - 1, n_s, d), _F32),
                   jax.ShapeDtypeStruct((kwb - 1, n_s, d), _F32)),
        grid=(n_blocks,),
        in_specs=[
            col(0), col(1), col(2), col(3),
            pl.BlockSpec((tr, d), st),
            pl.BlockSpec((kwa - 1, tr, d), st3),
            pl.BlockSpec((kwb - 1, tr, d), st3),
            pl.BlockSpec((kwa, d), full), pl.BlockSpec((1, d), full),
            pl.BlockSpec((n_heads, _HEAD, _HEAD), lambda i: (0, 0, 0),
                         pipeline_mode=pl.Buffered(1)), pl.BlockSpec((1, d), full),
            pl.BlockSpec((n_heads, _HEAD, _HEAD), lambda i: (0, 0, 0),
                         pipeline_mode=pl.Buffered(1)), pl.BlockSpec((1, d), full),
            pl.BlockSpec((1, d), full),
            pl.BlockSpec((kwb, d), full), pl.BlockSpec((1, d), full),
            pl.BlockSpec((1, d), full), pl.BlockSpec((1, d), full),
            pl.BlockSpec(memory_space=pl.ANY), pl.BlockSpec(memory_space=pl.ANY),
        ],
        out_specs=(
            pl.BlockSpec((tr, d), lambda i: (b0 + i, 0)),
            pl.BlockSpec((tr, d), lambda i: (b0 + i, 0)),
            pl.BlockSpec((tr, d), st),
            pl.BlockSpec((kwa - 1, tr, d), st3),
            pl.BlockSpec((kwb - 1, tr, d), st3),
        ),
        input_output_aliases={18: 0, 19: 1},
        compiler_params=pltpu.CompilerParams(
            dimension_semantics=("arbitrary",),
            vmem_limit_bytes=_vmem_limit(est)),
        name="branches_sample",
    )(proj, proj, proj, proj, state_h, state_a, state_b,
      p["conv_a_w"], p["conv_a_b"], p["w_r"], p["b_r"], p["w_i"], p["b_i"], p["lru_lambda"],
      p["conv_b_w"], p["conv_b_b"], p["ln_b_g"], p["ln_b_b"], ya, yb)


def _router_kernel(x_ref, g_ref, wr_ref, br_ref, xp_ref, te_ref, tg_ref, rk_ref, cnt_ref,
                   carry, *, n_exp):
    i = pl.program_id(0)

    @pl.when(i == 0)
    def _():
        carry[...] = jnp.zeros_like(carry)

    xn = _rms(x_ref[...], g_ref[...])
    xh = xn.astype(_BF16)
    xhf = xh.astype(_F32)
    half = xn.shape[1] // 2
    bits = lax.bitcast_convert_type(xhf, _U32)
    xp_ref[...] = (bits[:, 0:half] >> jnp.uint32(16)) | bits[:, half:2 * half]

    w = wr_ref[...]
    xl = (xn - xhf).astype(_BF16)
    wh = w.astype(_BF16)
    wl = (w - wh.astype(_F32)).astype(_BF16)
    logits = (jnp.dot(xh, wh, preferred_element_type=_F32)
              + (jnp.dot(xl, wh, preferred_element_type=_F32)
                 + jnp.dot(xh, wl, preferred_element_type=_F32))) + br_ref[...]

    tb = xn.shape[0]
    lane = lax.broadcasted_iota(jnp.int32, (tb, n_exp), 1)
    work = logits
    vals, idxs, hots = [], [], []
    for _ in range(_TOP_K):
        m = jnp.max(work, axis=-1, keepdims=True)
        idx = jnp.min(jnp.where(work == m, lane, n_exp), axis=-1, keepdims=True)
        hot = lane == idx
        vals.append(m); idxs.append(idx); hots.append(hot)
        work = jnp.where(hot, -jnp.inf, work)
    es = [jnp.exp(v - vals[0]) for v in vals]
    den = es[0] + es[1] + es[2] + es[3]
    tg_ref[...] = jnp.concatenate([e / den for e in es], axis=-1)
    te_ref[...] = jnp.concatenate(idxs, axis=-1)

    sel = (hots[0] | hots[1] | hots[2] | hots[3]).astype(_F32)
    r_i = lax.broadcasted_iota(jnp.int32, (tb, tb), 0)
    c_i = lax.broadcasted_iota(jnp.int32, (tb, tb), 1)
    tri = (c_i < r_i).astype(_BF16)
    before = jnp.dot(tri, sel.astype(_BF16), preferred_element_type=_F32) + carry[...]
    ranks = [jnp.sum(jnp.where(h, before, 0.0), axis=-1, keepdims=True) for h in hots]
    rk_ref[...] = jnp.concatenate(ranks, axis=-1).astype(jnp.int32)
    total = carry[...] + jnp.sum(sel, axis=0, keepdims=True)
    carry[...] = total
    cnt_ref[...] = total.astype(jnp.int32)


def _router(x2, g, w_router, b_router):
    rows, d = x2.shape
    n_exp = w_router.shape[1]
    tb = _largest_divisor(rows, 256, 8)
    tok = lambda i: (i, 0)
    full = lambda i: (0, 0)
    return pl.pallas_call(
        functools.partial(_router_kernel, n_exp=n_exp),
        out_shape=(jax.ShapeDtypeStruct((rows, d // 2), _U32),
                   jax.ShapeDtypeStruct((rows, _TOP_K), jnp.int32),
                   jax.ShapeDtypeStruct((rows, _TOP_K), _F32),
                   jax.ShapeDtypeStruct((rows, _TOP_K), jnp.int32),
                   jax.ShapeDtypeStruct((1, n_exp), jnp.int32)),
        grid=(rows // tb,),
        in_specs=[pl.BlockSpec((tb, d), tok), pl.BlockSpec((1, d), full),
                  pl.BlockSpec((d, n_exp), full), pl.BlockSpec((1, n_exp), full)],
        out_specs=(pl.BlockSpec((tb, d // 2), tok), pl.BlockSpec((tb, _TOP_K), tok),
                   pl.BlockSpec((tb, _TOP_K), tok), pl.BlockSpec((tb, _TOP_K), tok),
                   pl.BlockSpec((1, n_exp), full)),
        scratch_shapes=[pltpu.VMEM((1, n_exp), _F32)],
        compiler_params=pltpu.CompilerParams(
            dimension_semantics=("arbitrary",),
            vmem_limit_bytes=_vmem_limit(8 * tb * d * 4)),
        name="router",
    )(x2, g.reshape(1, d), w_router, b_router.reshape(1, n_exp))


def _dispatch_kernel(slot_ref, padrow_ref, padn_ref, x_ref, o_hbm, stage, zrow, sem,
                     *, blk, n_blocks, n_chunks):
    i = pl.program_id(0)

    def row_copy(p, src_row, dst_row):
        return pltpu.make_async_copy(stage.at[p, pl.ds(src_row, 1), :],
                                     o_hbm.at[pl.ds(dst_row, 1), :], sem.at[p])

    def drain(p, n):
        def body(t, carry):
            row_copy(p, 0, 0).wait()
            return carry
        lax.fori_loop(0, n, body, 0)

    def block_wait(p):
        for _ in range(_TOP_K):
            pltpu.make_async_copy(stage.at[p], o_hbm.at[pl.ds(0, blk), :], sem.at[p]).wait()

    for p in (0, 1):
        @pl.when(i % 2 == p)
        def _(p=p):
            @pl.when(i >= 2)
            def _():
                block_wait(p)

            stage[p] = x_ref[...]

            def issue(t, carry):
                tok = i * blk + t
                for k in range(_TOP_K):
                    row_copy(p, t, slot_ref[tok * _TOP_K + k]).start()
                return carry
            lax.fori_loop(0, blk, issue, 0)

    @pl.when(i == n_blocks - 1)
    def _():
        for p in range(min(n_blocks, 2)):
            block_wait(p)
        zrow[...] = jnp.zeros_like(zrow)

        def zero_fill(v, carry):
            n = padn_ref[v]

            def body(r, c2):
                pltpu.make_async_copy(zrow.at[pl.ds(0, 1), :],
                                      o_hbm.at[pl.ds(padrow_ref[v] + r, 1), :], sem.at[0]).start()
                return c2
            lax.fori_loop(0, n, body, 0)
            drain(0, n)
            return carry
        lax.fori_loop(0, n_chunks, zero_fill, 0)


def _dispatch(xp, slots, pad_row, pad_n, *, n_rows_out):
    n_tok, half = xp.shape
    blk = _largest_divisor(n_tok, 256, 8)
    n_blocks = n_tok // blk
    return pl.pallas_call(
        functools.partial(_dispatch_kernel, blk=blk, n_blocks=n_blocks, n_chunks=pad_n.shape[0]),
        out_shape=jax.ShapeDtypeStruct((n_rows_out, half), _U32),
        grid_spec=pltpu.PrefetchScalarGridSpec(
            num_scalar_prefetch=3,
            grid=(n_blocks,),
            in_specs=[pl.BlockSpec((blk, half), lambda i, s, pr, pn: (i, 0))],
            out_specs=pl.BlockSpec(memory_space=pl.ANY),
            scratch_shapes=[pltpu.VMEM((2, blk, half), _U32), pltpu.VMEM((8, half), _U32),
                            pltpu.SemaphoreType.DMA((2,))]),
        compiler_params=pltpu.CompilerParams(
            dimension_semantics=("arbitrary",),
            vmem_limit_bytes=_vmem_limit(6 * blk * half * 4)),
        name="dispatch",
    )(slots, pad_row, pad_n, xp)


def _combine_kernel(slot_ref, ys_hbm, g_ref, x_ref, gf_ref, om_ref, ot_ref, buf, sem,
                    *, tb, n_main_blocks):
    i = pl.program_id(0)
    n = pl.num_programs(0)

    def issue(b, p):
        base = b * (tb * _TOP_K)

        def body(r, carry):
            for k in range(_TOP_K):
                pltpu.make_async_copy(ys_hbm.at[pl.ds(slot_ref[base + r * _TOP_K + k], 1), :],
                                      buf.at[p, k, pl.ds(r, 1), :], sem.at[p]).start()
            return carry
        lax.fori_loop(0, tb, body, 0)

    @pl.when(i == 0)
    def _():
        issue(i, 0)

    for p in (0, 1):
        @pl.when(jnp.logical_and(i + 1 < n, (i + 1) % 2 == p))
        def _(p=p):
            issue(i + 1, p)

    for p in (0, 1):
        @pl.when(i % 2 == p)
        def _(p=p):
            for k in range(_TOP_K):
                pltpu.make_async_copy(ys_hbm.at[pl.ds(0, tb), :], buf.at[p, k], sem.at[p]).wait()
            g = g_ref[...]
            moe = g[:, 0:1] * buf[p, 0]
            for k in range(1, _TOP_K):
                moe = moe + g[:, k:k + 1] * buf[p, k]
            y = _rms(x_ref[...] + moe, gf_ref[...])

            @pl.when(i < n_main_blocks)
            def _():
                om_ref[...] = y

            @pl.when(i >= n_main_blocks)
            def _():
                ot_ref[...] = y


def _combine(ys, slots, gates, x2, g_final, *, n_main, tb):
    rows, d = x2.shape
    nmb = n_main // tb
    tok = lambda i, s: (i, 0)
    return pl.pallas_call(
        functools.partial(_combine_kernel, tb=tb, n_main_blocks=nmb),
        out_shape=(jax.ShapeDtypeStruct((n_main, d), _F32),
                   jax.ShapeDtypeStruct((rows - n_main, d), _F32)),
        grid_spec=pltpu.PrefetchScalarGridSpec(
            num_scalar_prefetch=1,
            grid=(rows // tb,),
            in_specs=[pl.BlockSpec(memory_space=pl.ANY),
                      pl.BlockSpec((tb, _TOP_K), tok),
                      pl.BlockSpec((tb, d), tok),
                      pl.BlockSpec((1, d), lambda i, s: (0, 0))],
            out_specs=(pl.BlockSpec((tb, d), lambda i, s: (jnp.minimum(i, nmb - 1), 0)),
                       pl.BlockSpec((tb, d), lambda i, s: (jnp.maximum(i - nmb, 0), 0))),
            scratch_shapes=[pltpu.VMEM((2, _TOP_K, tb, d), _F32),
                            pltpu.SemaphoreType.DMA((2,))]),
        compiler_params=pltpu.CompilerParams(
            dimension_semantics=("arbitrary",),
            vmem_limit_bytes=_vmem_limit((2 * _TOP_K + 10) * tb * d * 4)),
        name="combine",
    )(slots, ys, gates, x2, g_final.reshape(1, d))


def _moe_tables(counts, top_e, rank, *, tm, sub, quantum, n_chunks):
    n_exp = counts.shape[0]
    chunks_e = (counts + tm - 1) // tm
    chunk_end = jnp.cumsum(chunks_e)
    chunk_start = chunk_end - chunks_e
    n_used = chunk_end[-1]
    v = jnp.arange(n_chunks, dtype=jnp.int32)
    v_eff = jnp.minimum(v, n_used - 1)
    e_v = jnp.minimum(jnp.searchsorted(chunk_end, v_eff, side="right"), n_exp - 1).astype(jnp.int32)
    within = v_eff - chunk_start[e_v]
    rows_v = jnp.where(v < n_used, jnp.clip(counts[e_v] - within * tm, 0, tm), 0).astype(jnp.int32)
    pad_row = (v_eff * tm + rows_v).astype(jnp.int32)
    filled = jnp.maximum((rows_v + quantum - 1) // quantum * quantum, sub)
    pad_n = jnp.where(rows_v > 0, filled - rows_v, 0).astype(jnp.int32)
    slot = (chunk_start[top_e] * tm + rank).astype(jnp.int32)
    return (v_eff.astype(jnp.int32), e_v, rows_v), pad_row, pad_n, slot


def kernel(x_prompt, x_sample, state_lru_h, state_lru_conv, state_conf_conv, meta_tokens,
           norm_mix_g, w_in, conv_a_w, conv_a_b, w_r, b_r, w_i, b_i, lru_lambda,
           conv_b_w, conv_b_b, ln_b_g, ln_b_b, w_a_out, w_b_out, w_o, norm_ffn_g,
           w_router, b_router, w_gate, b_gate, w_up, b_up, w_down, b_down, norm_final_g):
    assert w_in.shape[0] == 1, "one layer"
    n_b, seq, d = x_prompt.shape
    n_s = x_sample.shape[0]
    n_meta = meta_tokens.shape[0]
    n_main = n_b * seq
    n_exp = w_router.shape[2]
    assert x_sample.shape[1] == 1 and n_s % _SUBLANES_BF16 == 0 and seq % _SUBLANES_BF16 == 0
    assert d % _HEAD == 0 and w_r.shape[2] == _HEAD
    assert n_meta % _SUBLANES_BF16 == 0 and (n_main + n_s) % n_meta == 0

    tr = _largest_divisor(n_main, 256, _SUBLANES_BF16)
    rows_tail = _round_up(n_s + n_meta, tr)
    rows = n_main + rows_tail
    meta_row0 = n_main + n_s
    tm_d = _largest_divisor(rows, 2112, 32)
    tm_o = _largest_divisor(rows, 1056, 32)
    sub_d = _largest_divisor(tm_d, 704, _SUBLANES_BF16)
    sub_o = _largest_divisor(tm_o, 704, _SUBLANES_BF16)
    tm_2 = _largest_divisor(rows, 1408, 32)
    sub_2 = _largest_divisor(tm_2, 704, _SUBLANES_BF16)
    tn = _largest_divisor(d, 512, 128)
    tn_2 = _largest_divisor(d, 256, 128)
    big = rows >= 4096
    tm_e = _EXPERT_CHUNK if big else 128
    sub_e = _EXPERT_SUB if big else 64
    q_e = _EXPERT_QUANTUM if big else 16
    n_pairs = rows * _TOP_K
    n_chunks = n_exp + n_pairs // tm_e

    x_tail = jnp.concatenate([x_sample.reshape(n_s, d), meta_tokens.astype(x_prompt.dtype),
                              jnp.zeros((rows_tail - n_s - n_meta, d), x_prompt.dtype)], axis=0)

    p = dict(conv_a_w=conv_a_w[0], conv_a_b=conv_a_b[0].reshape(1, d),
             w_r=w_r[0], b_r=b_r[0].reshape(1, d), w_i=w_i[0], b_i=b_i[0].reshape(1, d),
             lru_lambda=lru_lambda[0].reshape(1, d),
             conv_b_w=conv_b_w[0], conv_b_b=conv_b_b[0].reshape(1, d),
             ln_b_g=ln_b_g[0].reshape(1, d), ln_b_b=ln_b_b[0].reshape(1, d))

    x, xn = _rmsnorm_rows(x_prompt.reshape(n_main, d), x_tail, norm_mix_g[0], tr)
    proj = _gmm([xn], [w_in], [(0, 0)], _dense_tables(rows, tm_d),
                tm=tm_d, tn=tn, sub=sub_d, out_dtype=_F32, single_buffer_lhs=True,
                epilogue=lambda pr, ex: pr[0], name="in_proj")
    seq_args = dict(n_seq=n_b, t_len=seq, d=d, rows_total=rows, meta_row0=meta_row0, n_meta=n_meta)
    ya, h_p, ha_p = _branch_a_prompt(proj, p, **seq_args)
    yb, hb_p = _branch_b_prompt(proj, p, **seq_args)
    ya, yb, h_s, ha_s, hb_s = _branches_sample(
        proj, ya, yb, state_lru_h[0], jnp.swapaxes(state_lru_conv[0], 0, 1),
        jnp.swapaxes(state_conf_conv[0], 0, 1), p,
        row0=n_main, n_s=n_s, d=d, rows_total=rows)
    gate_off = 4 * d // tn_2
    merged = _gmm([ya, yb], [w_a_out, w_b_out], [(0, 0), (1, 1)],
                  _dense_tables(rows, tm_2), tm=tm_2, tn=tn_2, sub=sub_2,
                  out_dtype=_BF16, name="merge_proj", single_buffer_lhs=True,
                  extras=[(proj, gate_off), (proj, gate_off + d // tn_2)],
                  epilogue=lambda pr, ex: _sigmoid(ex[0]) * pr[0] + _sigmoid(ex[1]) * pr[1])
    x2 = _gmm([merged], [w_o], [(0, 0)], _dense_tables(rows, tm_o),
              tm=tm_o, tn=tn, sub=sub_o, out_dtype=_F32, name="out_proj",
              extras=[(x, 0)], epilogue=lambda pr, ex: ex[0] + pr[0])

    xp, top_e, gates, rank, counts = _router(x2, norm_ffn_g[0], w_router[0], b_router[0])
    tables, pad_row, pad_n, slot = _moe_tables(
        counts[0], top_e, rank, tm=tm_e, sub=sub_e, quantum=q_e, n_chunks=n_chunks)
    slots = slot.reshape(-1)
    xs = _dispatch(xp, slots, pad_row, pad_n, n_rows_out=n_chunks * tm_e)

    def swiglu(pr, ex):
        g = jnp.minimum(pr[0], _SWIGLU_LIMIT)
        u = jnp.clip(pr[1], -_SWIGLU_LIMIT, _SWIGLU_LIMIT)
        return g * _sigmoid(_SWIGLU_ALPHA * g) * (u + 1.0)

    hid = _gmm([xs], [w_gate[0], w_up[0]], [(0, 0), (0, 1)], tables, tm=tm_e, tn=tn_2,
               sub=sub_e, quantum=q_e, out_dtype=_BF16, name="expert_up", epilogue=swiglu,
               packed=True,
               biases=[b_gate[0][:, None, :], b_up[0][:, None, :]])
    ys = _gmm([hid], [w_down[0]], [(0, 0)], tables, tm=tm_e, tn=tn, sub=sub_e, quantum=q_e,
              out_dtype=_F32, name="expert_down", epilogue=lambda pr, ex: pr[0],
              biases=[b_down[0][:, None, :]])
    tb_c = _largest_divisor(math.gcd(n_main, rows_tail), 64, 8)
    y_main, y_tail = _combine(ys, slots, gates, x2, norm_final_g, n_main=n_main, tb=tb_c)

    return (y_main.reshape(n_b, seq, d), y_tail[:n_s].reshape(n_s, 1, d),
            h_p.reshape(1, n_b, d), ha_p[None], hb_p[None], h_s[None],
            jnp.swapaxes(ha_s, 0, 1)[None], jnp.swapaxes(hb_s, 0, 1)[None])
```

```python
import functools
import math

import jax
import jax.numpy as jnp
from jax import lax
from jax.experimental import pallas as pl
from jax.experimental.pallas import tpu as pltpu

_F32 = jnp.float32
_BF16 = jnp.bfloat16
_U32 = jnp.uint32

_EPS = 1e-6
_LRU_C = 8.0
_SWIGLU_LIMIT = 7.0
_SWIGLU_ALPHA = 1.702
_TOP_K = 4
_HEAD = 256

_V7X_VMEM_BYTES = 64 * 1024 * 1024
_VMEM_CAP = _V7X_VMEM_BYTES - 5 * 1024 * 1024
_SUBLANES_BF16 = 16
_SCAN_GROUP = 8

_TT_A_CAP = 512
_TT_B_CAP = 64
_EXPERT_CHUNK = 1280
_EXPERT_SUB = 256
_EXPERT_QUANTUM = 64


def _vmem_limit(estimate_bytes):
    return int(min(max(estimate_bytes + (6 << 20), 24 << 20), _VMEM_CAP))


def _largest_divisor(n, cap, mult):
    best = None
    for d in range(mult, min(n, cap) + 1, mult):
        if n % d == 0:
            best = d
    assert best is not None, (n, cap, mult)
    return best


def _round_up(n, m):
    return (n + m - 1) // m * m


def _sigmoid(x):
    return jax.nn.sigmoid(x)


def _gelu_tanh(x):
    c = math.sqrt(2.0 / math.pi)
    return 0.5 * x * (1.0 + jnp.tanh(c * (x + 0.044715 * (x * x * x))))


def _softplus(x):
    return jnp.maximum(x, 0.0) + jnp.log1p(jnp.exp(-jnp.abs(x)))


def _rms(x, g):
    ms = jnp.mean(x * x, axis=-1, keepdims=True)
    return (x * lax.rsqrt(ms + _EPS)) * g


def _rmsnorm_kernel(xm_ref, xt_ref, g_ref, x_ref, o_ref, *, n_main_blocks):
    i = pl.program_id(0)

    def emit(x):
        x_ref[...] = x
        o_ref[...] = _rms(x, g_ref[...]).astype(o_ref.dtype)

    @pl.when(i < n_main_blocks)
    def _():
        emit(xm_ref[...])

    @pl.when(i >= n_main_blocks)
    def _():
        emit(xt_ref[...])


def _rmsnorm_rows(x_main, x_tail, g, tr):
    n_main, d = x_main.shape
    rows = n_main + x_tail.shape[0]
    nmb = n_main // tr
    return pl.pallas_call(
        functools.partial(_rmsnorm_kernel, n_main_blocks=nmb),
        out_shape=(jax.ShapeDtypeStruct((rows, d), _F32),
                   jax.ShapeDtypeStruct((rows, d), _BF16)),
        grid=(rows // tr,),
        in_specs=[pl.BlockSpec((tr, d), lambda i: (jnp.minimum(i, nmb - 1), 0)),
                  pl.BlockSpec((tr, d), lambda i: (jnp.maximum(i - nmb, 0), 0)),
                  pl.BlockSpec((1, d), lambda i: (0, 0))],
        out_specs=(pl.BlockSpec((tr, d), lambda i: (i, 0)),
                   pl.BlockSpec((tr, d), lambda i: (i, 0))),
        compiler_params=pltpu.CompilerParams(
            dimension_semantics=("arbitrary",),
            vmem_limit_bytes=_vmem_limit(10 * tr * d * 4)),
        name="rmsnorm_rows",
    )(x_main, x_tail, g.reshape(1, d))


def _gmm_kernel(blk_ref, eid_ref, nrow_ref, *refs, n_lhs, n_w, has_bias, n_extra, pairs,
                tm, tn, sub, quantum, cast_rows, epilogue, packed, fused):
    del blk_ref, eid_ref
    pos = 0
    lhs_refs = refs[pos:pos + n_lhs]; pos += n_lhs
    w_refs = refs[pos:pos + n_w]; pos += n_w
    if has_bias:
        b_refs = refs[pos:pos + n_w]; pos += n_w
    ex_refs = refs[pos:pos + n_extra]; pos += n_extra
    o_ref = refs[pos]; pos += 1
    n_wbf = 1 if fused else n_w
    wbf_refs = refs[pos:pos + n_wbf]; pos += n_wbf
    if packed:
        lhs_refs = (refs[pos],)
        packed_ref = refs[0]

    nrows = nrow_ref[pl.program_id(0)]

    @pl.when(nrows > 0)
    def _():
        k_dim = w_refs[0].shape[0]

        def cast_body(c, carry):
            r = pl.multiple_of(c * cast_rows, cast_rows)
            for wi, w_ref in enumerate(w_refs):
                tile = w_ref[pl.ds(r, cast_rows), :].astype(_BF16)
                if fused:
                    wbf_refs[0][pl.ds(r, cast_rows), wi * tn:(wi + 1) * tn] = tile
                else:
                    wbf_refs[wi][pl.ds(r, cast_rows), :] = tile
            return carry

        lax.fori_loop(0, k_dim // cast_rows, cast_body, 0)

        if quantum is not None:
            per = sub // quantum
            n_q = jnp.maximum((nrows + quantum - 1) // quantum, per)

        if packed:
            half = k_dim // 2

            @pl.when(pl.program_id(1) == 0)
            def _():
                def unpack_body(q, carry):
                    r = pl.multiple_of(q * quantum, quantum)
                    w = packed_ref[pl.ds(r, quantum), :]
                    lo = lax.bitcast_convert_type(w << jnp.uint32(16), _F32)
                    hi = lax.bitcast_convert_type(w & jnp.uint32(0xFFFF0000), _F32)
                    lhs_refs[0][pl.ds(r, quantum), 0:half] = lo.astype(_BF16)
                    lhs_refs[0][pl.ds(r, quantum), half:k_dim] = hi.astype(_BF16)
                    return carry

                lax.fori_loop(0, n_q, unpack_body, 0)

        def compute(r, n):
            if fused:
                p = jnp.dot(lhs_refs[0][pl.ds(r, n), :], wbf_refs[0][...],
                            preferred_element_type=_F32)
                prods = [p[:, wi * tn:(wi + 1) * tn] for wi in range(n_w)]
            else:
                prods = [jnp.dot(lhs_refs[li][pl.ds(r, n), :], wbf_refs[wi][...],
                                 preferred_element_type=_F32) for (li, wi) in pairs]
            if has_bias:
                prods = [p + b_refs[wi][...] for p, (_, wi) in zip(prods, pairs)]
            extras = [e[pl.ds(r, n), :] for e in ex_refs]
            o_ref[pl.ds(r, n), :] = epilogue(prods, extras).astype(o_ref.dtype)

        def sub_body(s, carry):
            compute(pl.multiple_of(s * sub, sub), sub)
            return carry

        if quantum is None:
            lax.fori_loop(0, tm // sub, sub_body, 0)
        else:
            n_full = n_q // per
            n_tail = n_q - n_full * per
            n_plain = jnp.where(n_tail > 0, n_full - 1, n_full)

            def pair_body(s, carry):
                compute(pl.multiple_of(s * (2 * sub), 2 * sub), 2 * sub)
                return carry
            lax.fori_loop(0, n_plain // 2, pair_body, 0)

            @pl.when(n_plain % 2 == 1)
            def _():
                compute(pl.multiple_of((n_plain - 1) * sub, sub), sub)

            for tq in range(1, per):
                @pl.when(n_tail == tq)
                def _(tq=tq):
                    compute(pl.multiple_of((n_full - 1) * sub, sub), sub + tq * quantum)


def _gmm(lhs, ws, pairs, tables, *, tm, tn, sub, out_dtype, epilogue, name, quantum=None,
         biases=None, extras=(), packed=False, single_buffer_lhs=False):
    blk, eid, nrow = tables
    n_chunks = blk.shape[0]
    rows = lhs[0].shape[0]
    k_dim = ws[0].shape[1]
    n_out = ws[0].shape[2]
    nj = n_out // tn
    fused = len(lhs) == 1 and len(ws) > 1
    assert n_out % tn == 0 and tm % sub == 0 and rows % tm == 0
    assert quantum is None or sub % quantum == 0
    cast_rows = _largest_divisor(k_dim, 512, _SUBLANES_BF16)

    def jj(v, j, nrow_ref):
        return jnp.where(nrow_ref[v] > 0, j, nj - 1)

    lhs_mode = dict(pipeline_mode=pl.Buffered(1)) if single_buffer_lhs else {}
    in_specs = []
    for a in lhs:
        in_specs.append(pl.BlockSpec((tm, a.shape[1]), lambda v, j, b, e, n: (b[v], 0), **lhs_mode))
    for _ in ws:
        in_specs.append(pl.BlockSpec((None, k_dim, tn),
                                     lambda v, j, b, e, n: (e[v], 0, jj(v, j, n))))
    has_bias = biases is not None
    if has_bias:
        for _ in ws:
            in_specs.append(pl.BlockSpec((None, 1, tn),
                                         lambda v, j, b, e, n: (e[v], 0, jj(v, j, n))))
    for (_, off) in extras:
        in_specs.append(pl.BlockSpec(
            (tm, tn), lambda v, j, b, e, n, off=off: (b[v], off + jj(v, j, n))))
    out_spec = pl.BlockSpec((tm, tn), lambda v, j, b, e, n: (b[v], jj(v, j, n)))

    scratch = ([pltpu.VMEM((k_dim, len(ws) * tn), _BF16)] if fused
               else [pltpu.VMEM((k_dim, tn), _BF16) for _ in ws])
    if packed:
        scratch.append(pltpu.VMEM((tm, k_dim), _BF16))
    out_bytes = jnp.dtype(out_dtype).itemsize
    lhs_bufs = 1 if single_buffer_lhs else 2
    est = (lhs_bufs * len(lhs) * tm * k_dim * 2 + (tm * k_dim * 2 if packed else 0)
           + len(ws) * (2 * k_dim * tn * 4 + k_dim * tn * 2)
           + 2 * len(extras) * tm * tn * 4 + 2 * tm * tn * out_bytes
           + (3 + len(pairs)) * sub * tn * 4)
    kernel = functools.partial(
        _gmm_kernel, n_lhs=len(lhs), n_w=len(ws), has_bias=has_bias, n_extra=len(extras),
        pairs=tuple(pairs), tm=tm, tn=tn, sub=sub, quantum=quantum, cast_rows=cast_rows,
        epilogue=epilogue, packed=packed, fused=fused)
    args = list(lhs) + list(ws) + (list(biases) if has_bias else []) + [a for (a, _) in extras]
    return pl.pallas_call(
        kernel,
        out_shape=jax.ShapeDtypeStruct((rows, n_out), out_dtype),
        grid_spec=pltpu.PrefetchScalarGridSpec(
            num_scalar_prefetch=3,
            grid=(n_chunks, nj),
            in_specs=in_specs,
            out_specs=out_spec,
            scratch_shapes=scratch),
        compiler_params=pltpu.CompilerParams(
            dimension_semantics=("arbitrary", "arbitrary"),
            vmem_limit_bytes=_vmem_limit(est)),
        name=name,
    )(blk, eid, nrow, *args)


def _dense_tables(rows, tm):
    n = rows // tm
    return (jnp.arange(n, dtype=jnp.int32), jnp.zeros((n,), jnp.int32),
            jnp.full((n,), tm, jnp.int32))


def _lru_gates(ca, wr_ref, br_ref, wi_ref, bi_ref, lam_ref):
    cab = ca.astype(_BF16)
    n_heads = wr_ref.shape[0]
    zr, zi = [], []
    for hh in range(n_heads):
        c_h = cab[:, hh * _HEAD:(hh + 1) * _HEAD]
        zr.append(jnp.dot(c_h, wr_ref[hh].astype(_BF16), preferred_element_type=_F32))
        zi.append(jnp.dot(c_h, wi_ref[hh].astype(_BF16), preferred_element_type=_F32))
    zr = jnp.concatenate(zr, axis=-1) if n_heads > 1 else zr[0]
    zi = jnp.concatenate(zi, axis=-1) if n_heads > 1 else zi[0]
    r = _sigmoid(zr + br_ref[...])
    i = _sigmoid(zi + bi_ref[...])
    log_a = (-_LRU_C * r) * _softplus(-lam_ref[...])
    a = jnp.exp(log_a)
    u = jnp.sqrt(1.0 - a * a) * (i * ca)
    return a, u


def _branch_a_kernel(xm_ref, xa_ref, ga_ref, cw_ref, cb_ref, wr_ref, br_ref, wi_ref, bi_ref,
                     lam_ref, ya_ref, h_ref, hist_ref, buf, hcar, a_s, u_s, *, tt, kw, n_meta):
    t = pl.program_id(2)
    nt = pl.num_programs(2)
    hb = 8

    def run(x_tile, n):
        buf[hb:hb + n, :] = x_tile
        first = hb - (kw - 1)
        ca = cb_ref[...] + cw_ref[0:1, :] * buf[first:first + n, :]
        for k in range(1, kw):
            ca = ca + cw_ref[k:k + 1, :] * buf[first + k:first + k + n, :]
        tail = buf[hb + n - (kw - 1):hb + n, :]
        buf[first:hb, :] = tail
        a, u = _lru_gates(ca, wr_ref, br_ref, wi_ref, bi_ref, lam_ref)
        grouped = (n // _SCAN_GROUP, _SCAN_GROUP, a.shape[1])
        a = a.reshape(grouped)
        u = u.reshape(grouped)
        in_group = lax.broadcasted_iota(jnp.int32, grouped, 1)
        d = 1
        while d < _SCAN_GROUP:
            keep = in_group >= d
            a_sh = pltpu.roll(a, d, 1)
            u_sh = pltpu.roll(u, d, 1)
            u = jnp.where(keep, a * u_sh + u, u)
            a = jnp.where(keep, a * a_sh, a)
            d *= 2
        a_s[0:n, :] = a.reshape(n, grouped[2])
        u_s[0:n, :] = u.reshape(n, grouped[2])

        def group(g, carry):
            r = pl.multiple_of(g * _SCAN_GROUP, _SCAN_GROUP)
            h_g = u_s[pl.ds(r, _SCAN_GROUP), :] + a_s[pl.ds(r, _SCAN_GROUP), :] * carry
            u_s[pl.ds(r, _SCAN_GROUP), :] = h_g
            return h_g[_SCAN_GROUP - 1:_SCAN_GROUP, :]

        hcar[...] = lax.fori_loop(0, n // _SCAN_GROUP, group, hcar[...])
        return u_s[0:n, :], tail

    @pl.when(t == 0)
    def _():
        buf[0:hb, :] = jnp.zeros((hb, buf.shape[1]), _F32)
        hcar[...] = jnp.zeros_like(hcar)
        run(xm_ref[...], n_meta)

    h, tail = run(xa_ref[...], tt)
    ya_ref[...] = (h * _gelu_tanh(ga_ref[...])).astype(ya_ref.dtype)

    @pl.when(t == nt - 1)
    def _():
        h_ref[...] = h[tt - 1:tt, :]
        hist_ref[...] = tail


def _branch_a_prompt(proj, p, *, n_seq, t_len, d, rows_total, meta_row0, n_meta):
    kw = p["conv_a_w"].shape[0]
    tt = _largest_divisor(t_len, _TT_A_CAP, _SUBLANES_BF16)
    cw = _largest_divisor(d, 512, _HEAD)
    nt, nc = t_len // tt, d // cw
    hpc = cw // _HEAD
    ga_off = d // cw
    mblk = meta_row0 // n_meta
    vec = lambda b, c, t: (0, c)
    kernel = functools.partial(_branch_a_kernel, tt=tt, kw=kw, n_meta=n_meta)
    return pl.pallas_call(
        kernel,
        out_shape=(jax.ShapeDtypeStruct((rows_total, d), _BF16),
                   jax.ShapeDtypeStruct((n_seq, 1, d), _F32),
                   jax.ShapeDtypeStruct((n_seq, kw - 1, d), _F32)),
        grid=(n_seq, nc, nt),
        in_specs=[
            pl.BlockSpec((n_meta, cw), lambda b, c, t: (mblk, c)),
            pl.BlockSpec((tt, cw), lambda b, c, t: (b * nt + t, c)),
            pl.BlockSpec((tt, cw), lambda b, c, t: (b * nt + t, ga_off + c)),
            pl.BlockSpec((kw, cw), vec),
            pl.BlockSpec((1, cw), vec),
            pl.BlockSpec((hpc, _HEAD, _HEAD), lambda b, c, t: (c, 0, 0)),
            pl.BlockSpec((1, cw), vec),
            pl.BlockSpec((hpc, _HEAD, _HEAD), lambda b, c, t: (c, 0, 0)),
            pl.BlockSpec((1, cw), vec),
            pl.BlockSpec((1, cw), vec),
        ],
        out_specs=(
            pl.BlockSpec((tt, cw), lambda b, c, t: (b * nt + t, c)),
            pl.BlockSpec((None, 1, cw), lambda b, c, t: (b, 0, c)),
            pl.BlockSpec((None, kw - 1, cw), lambda b, c, t: (b, 0, c)),
        ),
        scratch_shapes=[pltpu.VMEM((8 + max(tt, n_meta), cw), _F32), pltpu.VMEM((1, cw), _F32),
                        pltpu.VMEM((max(tt, n_meta), cw), _F32),
                        pltpu.VMEM((max(tt, n_meta), cw), _F32)],
        compiler_params=pltpu.CompilerParams(
            dimension_semantics=("arbitrary", "arbitrary", "arbitrary"),
            vmem_limit_bytes=_vmem_limit(24 * tt * cw * 4)),
        name="branch_a_prompt",
    )(proj, proj, proj, p["conv_a_w"], p["conv_a_b"], p["w_r"], p["b_r"], p["w_i"], p["b_i"],
      p["lru_lambda"])


def _layernorm_silu(cb, g, b):
    mu = jnp.mean(cb, axis=-1, keepdims=True)
    xc = cb - mu
    y = xc * lax.rsqrt(jnp.mean(xc * xc, axis=-1, keepdims=True) + _EPS)
    y = y * g + b
    return y * _sigmoid(y)


def _branch_b_kernel(xgm_ref, gtm_ref, xg_ref, gt_ref, cw_ref, cb_ref, lg_ref, lb_ref,
                     yb_ref, hist_ref, buf, cbuf, shifted, *, tt, kw, lane_chunk, n_meta):
    t = pl.program_id(1)
    nt = pl.num_programs(1)
    hb = _round_up(kw - 1, 8)
    d = buf.shape[1]

    @pl.when(t == 0)
    def _():
        buf[0:hb, :] = jnp.zeros((hb, d), _F32)
        buf[hb - n_meta:hb, :] = xgm_ref[...] * _sigmoid(gtm_ref[...])

    buf[hb:hb + tt, :] = xg_ref[...] * _sigmoid(gt_ref[...])
    first = hb - (kw - 1)
    for c0 in range(0, d, lane_chunk):
        cs = slice(c0, c0 + lane_chunk)
        acc = jnp.broadcast_to(cb_ref[:, cs], (tt, lane_chunk))
        for res in range(_SCAN_GROUP):
            taps = [k for k in range(kw) if (first + k) % _SCAN_GROUP == res]
            if not taps:
                continue
            span = _SCAN_GROUP * max((first + k) // _SCAN_GROUP for k in taps) + tt
            shifted[0:span, :] = buf[res:res + span, cs]
            for k in taps:
                q0 = (first + k) // _SCAN_GROUP * _SCAN_GROUP
                acc = acc + cw_ref[k:k + 1, cs] * shifted[q0:q0 + tt, :]
        cbuf[:, cs] = acc
    tail = buf[hb + tt - (kw - 1):hb + tt, :]
    buf[first:hb, :] = tail
    yb_ref[...] = _layernorm_silu(cbuf[...], lg_ref[...], lb_ref[...]).astype(yb_ref.dtype)

    @pl.when(t == nt - 1)
    def _():
        hist_ref[...] = tail


def _branch_b_prompt(proj, p, *, n_seq, t_len, d, rows_total, meta_row0, n_meta):
    kw = p["conv_b_w"].shape[0]
    tt = _largest_divisor(t_len, _TT_B_CAP, _SUBLANES_BF16)
    nt = t_len // tt
    hb = _round_up(kw - 1, 8)
    assert n_meta <= hb and tt >= kw - 1
    mblk = meta_row0 // n_meta
    vec = lambda b, t: (0, 0)
    kernel = functools.partial(_branch_b_kernel, tt=tt, kw=kw, n_meta=n_meta,
                               lane_chunk=_largest_divisor(d, 256, 128))
    return pl.pallas_call(
        kernel,
        out_shape=(jax.ShapeDtypeStruct((rows_total, d), _BF16),
                   jax.ShapeDtypeStruct((n_seq, kw - 1, d), _F32)),
        grid=(n_seq, nt),
        in_specs=[
            pl.BlockSpec((n_meta, d), lambda b, t: (mblk, 2)),
            pl.BlockSpec((n_meta, d), lambda b, t: (mblk, 3)),
            pl.BlockSpec((tt, d), lambda b, t: (b * nt + t, 2)),
            pl.BlockSpec((tt, d), lambda b, t: (b * nt + t, 3)),
            pl.BlockSpec((kw, d), vec),
            pl.BlockSpec((1, d), vec),
            pl.BlockSpec((1, d), vec),
            pl.BlockSpec((1, d), vec),
        ],
        out_specs=(
            pl.BlockSpec((tt, d), lambda b, t: (b * nt + t, 0)),
            pl.BlockSpec((None, kw - 1, d), lambda b, t: (b, 0, 0)),
        ),
        scratch_shapes=[pltpu.VMEM((hb + tt, d), _F32), pltpu.VMEM((tt, d), _F32),
                        pltpu.VMEM((hb + tt, _largest_divisor(d, 256, 128)), _F32)],
        compiler_params=pltpu.CompilerParams(
            dimension_semantics=("arbitrary", "arbitrary"),
            vmem_limit_bytes=_vmem_limit(16 * (hb + tt) * d * 4)),
        name="branch_b_prompt",
    )(proj, proj, proj, proj, p["conv_b_w"], p["conv_b_b"], p["ln_b_g"], p["ln_b_b"])


def _sample_kernel(xa_ref, ga_ref, xg_ref, gt_ref, sh_ref, sa_ref, sb_ref,
                   caw_ref, cab_ref, wr_ref, br_ref, wi_ref, bi_ref, lam_ref,
                   cbw_ref, cbb_ref, lg_ref, lb_ref, ya_in, yb_in,
                   ya_ref, yb_ref, oh_ref, oa_ref, ob_ref, *, n_real, kwa, kwb):
    del ya_in, yb_in
    i = pl.program_id(0)

    def conv_step(hist_ref, new_ref, x_new, w_ref, b_ref, kw):
        acc = b_ref[...] + w_ref[kw - 1:kw, :] * x_new
        for k in range(kw - 1):
            h_k = hist_ref[k]
            acc = acc + w_ref[k:k + 1, :] * h_k
            if k >= 1:
                new_ref[k - 1] = h_k
        new_ref[kw - 2] = x_new
        return acc

    @pl.when(i < n_real)
    def _():
        glu = xg_ref[...] * _sigmoid(gt_ref[...])
        cb = conv_step(sb_ref, ob_ref, glu, cbw_ref, cbb_ref, kwb)
        ca = conv_step(sa_ref, oa_ref, xa_ref[...], caw_ref, cab_ref, kwa)
        a, u = _lru_gates(ca, wr_ref, br_ref, wi_ref, bi_ref, lam_ref)
        h = a * sh_ref[...] + u
        oh_ref[...] = h
        ya_ref[...] = (h * _gelu_tanh(ga_ref[...])).astype(ya_ref.dtype)
        yb_ref[...] = _layernorm_silu(cb, lg_ref[...], lb_ref[...]).astype(yb_ref.dtype)

    @pl.when(i >= n_real)
    def _():
        ya_ref[...] = jnp.zeros_like(ya_ref)
        yb_ref[...] = jnp.zeros_like(yb_ref)


def _branches_sample(proj, ya, yb, state_h, state_a, state_b, p, *, row0, n_s, d, rows_total):
    tr = _SUBLANES_BF16
    kwa, kwb = p["conv_a_w"].shape[0], p["conv_b_w"].shape[0]
    n_real = n_s // tr
    n_blocks = (rows_total - row0) // tr
    b0 = row0 // tr
    n_heads = d // _HEAD
    full = lambda i: (0, 0)
    st = lambda i: (jnp.minimum(i, n_real - 1), 0)
    st3 = lambda i: (0, jnp.minimum(i, n_real - 1), 0)
    kernel = functools.partial(_sample_kernel, n_real=n_real, kwa=kwa, kwb=kwb)
    col = lambda c: pl.BlockSpec((tr, d), lambda i, c=c: (b0 + i, c))
    est = (4 * tr * (kwa + kwb) * d * 4 + 2 * n_heads * _HEAD * _HEAD * 4
           + 24 * tr * d * 4 + 2 * (kwb + 16) * d * 4)
    return pl.pallas_call(
        kernel,
        out_shape=(jax.ShapeDtypeStruct((rows_total, d), _BF16),
                   jax.ShapeDtypeStruct((rows_total, d), _BF16),
                   jax.ShapeDtypeStruct((n_s, d), _F32),
                   jax.ShapeDtypeStruct((kwa - 1, n_s, d), _F32),
                   jax.ShapeDtypeStruct((kwb - 1, n_s, d), _F32)),
        grid=(n_blocks,),
        in_specs=[
            col(0), col(1), col(2), col(3),
            pl.BlockSpec((tr, d), st),
            pl.BlockSpec((kwa - 1, tr, d), st3),
            pl.BlockSpec((kwb - 1, tr, d), st3),
            pl.BlockSpec((kwa, d), full), pl.BlockSpec((1, d), full),
            pl.BlockSpec((n_heads, _HEAD, _HEAD), lambda i: (0, 0, 0),
                         pipeline_mode=pl.Buffered(1)), pl.BlockSpec((1, d), full),
            pl.BlockSpec((n_heads, _HEAD, _HEAD), lambda i: (0, 0, 0),
                         pipeline_mode=pl.Buffered(1)), pl.BlockSpec((1, d), full),
            pl.BlockSpec((1, d), full),
            pl.BlockSpec((kwb, d), full), pl.BlockSpec((1, d), full),
            pl.BlockSpec((1, d), full), pl.BlockSpec((1, d), full),
            pl.BlockSpec(memory_space=pl.ANY), pl.BlockSpec(memory_space=pl.ANY),
        ],
        out_specs=(
            pl.BlockSpec((tr, d), lambda i: (b0 + i, 0)),
            pl.BlockSpec((tr, d), lambda i: (b0 + i, 0)),
            pl.BlockSpec((tr, d), st),
            pl.BlockSpec((kwa - 1, tr, d), st3),
            pl.BlockSpec((kwb - 1, tr, d), st3),
        ),
        input_output_aliases={18: 0, 19: 1},
        compiler_params=pltpu.CompilerParams(
            dimension_semantics=("arbitrary",),
            vmem_limit_bytes=_vmem_limit(est)),
        name="branches_sample",
    )(proj, proj, proj, proj, state_h, state_a, state_b,
      p["conv_a_w"], p["conv_a_b"], p["w_r"], p["b_r"], p["w_i"], p["b_i"], p["lru_lambda"],
      p["conv_b_w"], p["conv_b_b"], p["ln_b_g"], p["ln_b_b"], ya, yb)


def _router_kernel(x_ref, g_ref, wr_ref, br_ref, xp_ref, te_ref, tg_ref, rk_ref, cnt_ref,
                   carry, *, n_exp):
    i = pl.program_id(0)

    @pl.when(i == 0)
    def _():
        carry[...] = jnp.zeros_like(carry)

    xn = _rms(x_ref[...], g_ref[...])
    xh = xn.astype(_BF16)
    xhf = xh.astype(_F32)
    half = xn.shape[1] // 2
    bits = lax.bitcast_convert_type(xhf, _U32)
    xp_ref[...] = (bits[:, 0:half] >> jnp.uint32(16)) | bits[:, half:2 * half]

    w = wr_ref[...]
    xl = (xn - xhf).astype(_BF16)
    wh = w.astype(_BF16)
    wl = (w - wh.astype(_F32)).astype(_BF16)
    logits = (jnp.dot(xh, wh, preferred_element_type=_F32)
              + (jnp.dot(xl, wh, preferred_element_type=_F32)
                 + jnp.dot(xh, wl, preferred_element_type=_F32))) + br_ref[...]

    tb = xn.shape[0]
    lane = lax.broadcasted_iota(jnp.int32, (tb, n_exp), 1)
    work = logits
    vals, idxs, hots = [], [], []
    for _ in range(_TOP_K):
        m = jnp.max(work, axis=-1, keepdims=True)
        idx = jnp.min(jnp.where(work == m, lane, n_exp), axis=-1, keepdims=True)
        hot = lane == idx
        vals.append(m); idxs.append(idx); hots.append(hot)
        work = jnp.where(hot, -jnp.inf, work)
    es = [jnp.exp(v - vals[0]) for v in vals]
    den = es[0] + es[1] + es[2] + es[3]
    tg_ref[...] = jnp.concatenate([e / den for e in es], axis=-1)
    te_ref[...] = jnp.concatenate(idxs, axis=-1)

    sel = (hots[0] | hots[1] | hots[2] | hots[3]).astype(_F32)
    r_i = lax.broadcasted_iota(jnp.int32, (tb, tb), 0)
    c_i = lax.broadcasted_iota(jnp.int32, (tb, tb), 1)
    tri = (c_i < r_i).astype(_BF16)
    before = jnp.dot(tri, sel.astype(_BF16), preferred_element_type=_F32) + carry[...]
    ranks = [jnp.sum(jnp.where(h, before, 0.0), axis=-1, keepdims=True) for h in hots]
    rk_ref[...] = jnp.concatenate(ranks, axis=-1).astype(jnp.int32)
    total = carry[...] + jnp.sum(sel, axis=0, keepdims=True)
    carry[...] = total
    cnt_ref[...] = total.astype(jnp.int32)


def _router(x2, g, w_router, b_router):
    rows, d = x2.shape
    n_exp = w_router.shape[1]
    tb = _largest_divisor(rows, 256, 8)
    tok = lambda i: (i, 0)
    full = lambda i: (0, 0)
    return pl.pallas_call(
        functools.partial(_router_kernel, n_exp=n_exp),
        out_shape=(jax.ShapeDtypeStruct((rows, d // 2), _U32),
                   jax.ShapeDtypeStruct((rows, _TOP_K), jnp.int32),
                   jax.ShapeDtypeStruct((rows, _TOP_K), _F32),
                   jax.ShapeDtypeStruct((rows, _TOP_K), jnp.int32),
                   jax.ShapeDtypeStruct((1, n_exp), jnp.int32)),
        grid=(rows // tb,),
        in_specs=[pl.BlockSpec((tb, d), tok), pl.BlockSpec((1, d), full),
                  pl.BlockSpec((d, n_exp), full), pl.BlockSpec((1, n_exp), full)],
        out_specs=(pl.BlockSpec((tb, d // 2), tok), pl.BlockSpec((tb, _TOP_K), tok),
                   pl.BlockSpec((tb, _TOP_K), tok), pl.BlockSpec((tb, _TOP_K), tok),
                   pl.BlockSpec((1, n_exp), full)),
        scratch_shapes=[pltpu.VMEM((1, n_exp), _F32)],
        compiler_params=pltpu.CompilerParams(
            dimension_semantics=("arbitrary",),
            vmem_limit_bytes=_vmem_limit(8 * tb * d * 4)),
        name="router",
    )(x2, g.reshape(1, d), w_router, b_router.reshape(1, n_exp))


def _dispatch_kernel(slot_ref, padrow_ref, padn_ref, x_ref, o_hbm, stage, zrow, sem,
                     *, blk, n_blocks, n_chunks):
    i = pl.program_id(0)

    def row_copy(p, src_row, dst_row):
        return pltpu.make_async_copy(stage.at[p, pl.ds(src_row, 1), :],
                                     o_hbm.at[pl.ds(dst_row, 1), :], sem.at[p])

    def drain(p, n):
        def body(t, carry):
            row_copy(p, 0, 0).wait()
            return carry
        lax.fori_loop(0, n, body, 0)

    def block_wait(p):
        for _ in range(_TOP_K):
            pltpu.make_async_copy(stage.at[p], o_hbm.at[pl.ds(0, blk), :], sem.at[p]).wait()

    for p in (0, 1):
        @pl.when(i % 2 == p)
        def _(p=p):
            @pl.when(i >= 2)
            def _():
                block_wait(p)

            stage[p] = x_ref[...]

            def issue(t, carry):
                tok = i * blk + t
                for k in range(_TOP_K):
                    row_copy(p, t, slot_ref[tok * _TOP_K + k]).start()
                return carry
            lax.fori_loop(0, blk, issue, 0)

    @pl.when(i == n_blocks - 1)
    def _():
        for p in range(min(n_blocks, 2)):
            block_wait(p)
        zrow[...] = jnp.zeros_like(zrow)

        def zero_fill(v, carry):
            n = padn_ref[v]

            def body(r, c2):
                pltpu.make_async_copy(zrow.at[pl.ds(0, 1), :],
                                      o_hbm.at[pl.ds(padrow_ref[v] + r, 1), :], sem.at[0]).start()
                return c2
            lax.fori_loop(0, n, body, 0)
            drain(0, n)
            return carry
        lax.fori_loop(0, n_chunks, zero_fill, 0)


def _dispatch(xp, slots, pad_row, pad_n, *, n_rows_out):
    n_tok, half = xp.shape
    blk = _largest_divisor(n_tok, 256, 8)
    n_blocks = n_tok // blk
    return pl.pallas_call(
        functools.partial(_dispatch_kernel, blk=blk, n_blocks=n_blocks, n_chunks=pad_n.shape[0]),
        out_shape=jax.ShapeDtypeStruct((n_rows_out, half), _U32),
        grid_spec=pltpu.PrefetchScalarGridSpec(
            num_scalar_prefetch=3,
            grid=(n_blocks,),
            in_specs=[pl.BlockSpec((blk, half), lambda i, s, pr, pn: (i, 0))],
            out_specs=pl.BlockSpec(memory_space=pl.ANY),
            scratch_shapes=[pltpu.VMEM((2, blk, half), _U32), pltpu.VMEM((8, half), _U32),
                            pltpu.SemaphoreType.DMA((2,))]),
        compiler_params=pltpu.CompilerParams(
            dimension_semantics=("arbitrary",),
            vmem_limit_bytes=_vmem_limit(6 * blk * half * 4)),
        name="dispatch",
    )(slots, pad_row, pad_n, xp)


def _combine_kernel(slot_ref, ys_hbm, g_ref, x_ref, gf_ref, om_ref, ot_ref, buf, sem,
                    *, tb, n_main_blocks):
    i = pl.program_id(0)
    n = pl.num_programs(0)

    def issue(b, p):
        base = b * (tb * _TOP_K)

        def body(r, carry):
            for k in range(_TOP_K):
                pltpu.make_async_copy(ys_hbm.at[pl.ds(slot_ref[base + r * _TOP_K + k], 1), :],
                                      buf.at[p, k, pl.ds(r, 1), :], sem.at[p]).start()
            return carry
        lax.fori_loop(0, tb, body, 0)

    @pl.when(i == 0)
    def _():
        issue(i, 0)

    for p in (0, 1):
        @pl.when(jnp.logical_and(i + 1 < n, (i + 1) % 2 == p))
        def _(p=p):
            issue(i + 1, p)

    for p in (0, 1):
        @pl.when(i % 2 == p)
        def _(p=p):
            for k in range(_TOP_K):
                pltpu.make_async_copy(ys_hbm.at[pl.ds(0, tb), :], buf.at[p, k], sem.at[p]).wait()
            g = g_ref[...]
            moe = g[:, 0:1] * buf[p, 0]
            for k in range(1, _TOP_K):
                moe = moe + g[:, k:k + 1] * buf[p, k]
            y = _rms(x_ref[...] + moe, gf_ref[...])

            @pl.when(i < n_main_blocks)
            def _():
                om_ref[...] = y

            @pl.when(i >= n_main_blocks)
            def _():
                ot_ref[...] = y


def _combine(ys, slots, gates, x2, g_final, *, n_main, tb):
    rows, d = x2.shape
    nmb = n_main // tb
    tok = lambda i, s: (i, 0)
    return pl.pallas_call(
        functools.partial(_combine_kernel, tb=tb, n_main_blocks=nmb),
        out_shape=(jax.ShapeDtypeStruct((n_main, d), _F32),
                   jax.ShapeDtypeStruct((rows - n_main, d), _F32)),
        grid_spec=pltpu.PrefetchScalarGridSpec(
            num_scalar_prefetch=1,
            grid=(rows // tb,),
            in_specs=[pl.BlockSpec(memory_space=pl.ANY),
                      pl.BlockSpec((tb, _TOP_K), tok),
                      pl.BlockSpec((tb, d), tok),
                      pl.BlockSpec((1, d), lambda i, s: (0, 0))],
            out_specs=(pl.BlockSpec((tb, d), lambda i, s: (jnp.minimum(i, nmb - 1), 0)),
                       pl.BlockSpec((tb, d), lambda i, s: (jnp.maximum(i - nmb, 0), 0))),
            scratch_shapes=[pltpu.VMEM((2, _TOP_K, tb, d), _F32),
                            pltpu.SemaphoreType.DMA((2,))]),
        compiler_params=pltpu.CompilerParams(
            dimension_semantics=("arbitrary",),
            vmem_limit_bytes=_vmem_limit((2 * _TOP_K + 10) * tb * d * 4)),
        name="combine",
    )(slots, ys, gates, x2, g_final.reshape(1, d))


def _moe_tables(counts, top_e, rank, *, tm, sub, quantum, n_chunks):
    n_exp = counts.shape[0]
    chunks_e = (counts + tm - 1) // tm
    chunk_end = jnp.cumsum(chunks_e)
    chunk_start = chunk_end - chunks_e
    n_used = chunk_end[-1]
    v = jnp.arange(n_chunks, dtype=jnp.int32)
    v_eff = jnp.minimum(v, n_used - 1)
    e_v = jnp.minimum(jnp.searchsorted(chunk_end, v_eff, side="right"), n_exp - 1).astype(jnp.int32)
    within = v_eff - chunk_start[e_v]
    rows_v = jnp.where(v < n_used, jnp.clip(counts[e_v] - within * tm, 0, tm), 0).astype(jnp.int32)
    pad_row = (v_eff * tm + rows_v).astype(jnp.int32)
    filled = jnp.maximum((rows_v + quantum - 1) // quantum * quantum, sub)
    pad_n = jnp.where(rows_v > 0, filled - rows_v, 0).astype(jnp.int32)
    slot = (chunk_start[top_e] * tm + rank).astype(jnp.int32)
    return (v_eff.astype(jnp.int32), e_v, rows_v), pad_row, pad_n, slot


def kernel(x_prompt, x_sample, state_lru_h, state_lru_conv, state_conf_conv, meta_tokens,
           norm_mix_g, w_in, conv_a_w, conv_a_b, w_r, b_r, w_i, b_i, lru_lambda,
           conv_b_w, conv_b_b, ln_b_g, ln_b_b, w_a_out, w_b_out, w_o, norm_ffn_g,
           w_router, b_router, w_gate, b_gate, w_up, b_up, w_down, b_down, norm_final_g):
    assert w_in.shape[0] == 1, "one layer"
    n_b, seq, d = x_prompt.shape
    n_s = x_sample.shape[0]
    n_meta = meta_tokens.shape[0]
    n_main = n_b * seq
    n_exp = w_router.shape[2]
    assert x_sample.shape[1] == 1 and n_s % _SUBLANES_BF16 == 0 and seq % _SUBLANES_BF16 == 0
    assert d % _HEAD == 0 and w_r.shape[2] == _HEAD
    assert n_meta % _SUBLANES_BF16 == 0 and (n_main + n_s) % n_meta == 0

    tr = _largest_divisor(n_main, 256, _SUBLANES_BF16)
    rows_tail = _round_up(n_s + n_meta, tr)
    rows = n_main + rows_tail
    meta_row0 = n_main + n_s
    tm_d = _largest_divisor(rows, 2112, 32)
    tm_o = _largest_divisor(rows, 1056, 32)
    sub_d = _largest_divisor(tm_d, 704, _SUBLANES_BF16)
    sub_o = _largest_divisor(tm_o, 704, _SUBLANES_BF16)
    tm_2 = _largest_divisor(rows, 1408, 32)
    sub_2 = _largest_divisor(tm_2, 704, _SUBLANES_BF16)
    tn = _largest_divisor(d, 512, 128)
    tn_2 = _largest_divisor(d, 256, 128)
    big = rows >= 4096
    tm_e = _EXPERT_CHUNK if big else 128
    sub_e = _EXPERT_SUB if big else 64
    q_e = _EXPERT_QUANTUM if big else 16
    n_pairs = rows * _TOP_K
    n_chunks = n_exp + n_pairs // tm_e

    x_tail = jnp.concatenate([x_sample.reshape(n_s, d), meta_tokens.astype(x_prompt.dtype),
                              jnp.zeros((rows_tail - n_s - n_meta, d), x_prompt.dtype)], axis=0)

    p = dict(conv_a_w=conv_a_w[0], conv_a_b=conv_a_b[0].reshape(1, d),
             w_r=w_r[0], b_r=b_r[0].reshape(1, d), w_i=w_i[0], b_i=b_i[0].reshape(1, d),
             lru_lambda=lru_lambda[0].reshape(1, d),
             conv_b_w=conv_b_w[0], conv_b_b=conv_b_b[0].reshape(1, d),
             ln_b_g=ln_b_g[0].reshape(1, d), ln_b_b=ln_b_b[0].reshape(1, d))

    x, xn = _rmsnorm_rows(x_prompt.reshape(n_main, d), x_tail, norm_mix_g[0], tr)
    proj = _gmm([xn], [w_in], [(0, 0)], _dense_tables(rows, tm_d),
                tm=tm_d, tn=tn, sub=sub_d, out_dtype=_F32, single_buffer_lhs=True,
                epilogue=lambda pr, ex: pr[0], name="in_proj")
    seq_args = dict(n_seq=n_b, t_len=seq, d=d, rows_total=rows, meta_row0=meta_row0, n_meta=n_meta)
    ya, h_p, ha_p = _branch_a_prompt(proj, p, **seq_args)
    yb, hb_p = _branch_b_prompt(proj, p, **seq_args)
    ya, yb, h_s, ha_s, hb_s = _branches_sample(
        proj, ya, yb, state_lru_h[0], jnp.swapaxes(state_lru_conv[0], 0, 1),
        jnp.swapaxes(state_conf_conv[0], 0, 1), p,
        row0=n_main, n_s=n_s, d=d, rows_total=rows)
    gate_off = 4 * d // tn_2
    merged = _gmm([ya, yb], [w_a_out, w_b_out], [(0, 0), (1, 1)],
                  _dense_tables(rows, tm_2), tm=tm_2, tn=tn_2, sub=sub_2,
                  out_dtype=_BF16, name="merge_proj", single_buffer_lhs=True,
                  extras=[(proj, gate_off), (proj, gate_off + d // tn_2)],
                  epilogue=lambda pr, ex: _sigmoid(ex[0]) * pr[0] + _sigmoid(ex[1]) * pr[1])
    x2 = _gmm([merged], [w_o], [(0, 0)], _dense_tables(rows, tm_o),
              tm=tm_o, tn=tn, sub=sub_o, out_dtype=_F32, name="out_proj",
              extras=[(x, 0)], epilogue=lambda pr, ex: ex[0] + pr[0])

    xp, top_e, gates, rank, counts = _router(x2, norm_ffn_g[0], w_router[0], b_router[0])
    tables, pad_row, pad_n, slot = _moe_tables(
        counts[0], top_e, rank, tm=tm_e, sub=sub_e, quantum=q_e, n_chunks=n_chunks)
    slots = slot.reshape(-1)
    xs = _dispatch(xp, slots, pad_row, pad_n, n_rows_out=n_chunks * tm_e)

    def swiglu(pr, ex):
        g = jnp.minimum(pr[0], _SWIGLU_LIMIT)
        u = jnp.clip(pr[1], -_SWIGLU_LIMIT, _SWIGLU_LIMIT)
        return g * _sigmoid(_SWIGLU_ALPHA * g) * (u + 1.0)

    hid = _gmm([xs], [w_gate[0], w_up[0]], [(0, 0), (0, 1)], tables, tm=tm_e, tn=tn_2,
               sub=sub_e, quantum=q_e, out_dtype=_BF16, name="expert_up", epilogue=swiglu,
               packed=True,
               biases=[b_gate[0][:, None, :], b_up[0][:, None, :]])
    ys = _gmm([hid], [w_down[0]], [(0, 0)], tables, tm=tm_e, tn=tn, sub=sub_e, quantum=q_e,
              out_dtype=_F32, name="expert_down", epilogue=lambda pr, ex: pr[0],
              biases=[b_down[0][:, None, :]])
    tb_c = _largest_divisor(math.gcd(n_main, rows_tail), 64, 8)
    y_main, y_tail = _combine(ys, slots, gates, x2, norm_final_g, n_main=n_main, tb=tb_c)

    return (y_main.reshape(n_b, seq, d), y_tail[:n_s].reshape(n_s, 1, d),
            h_p.reshape(1, n_b, d), ha_p[None], hb_p[None], h_s[None],
            jnp.swapaxes(ha_s, 0, 1)[None], jnp.swapaxes(hb_s, 0, 1)[None])
```

```python
import functools
import math

import jax
import jax.numpy as jnp
from jax import lax
from jax.experimental import pallas as pl
from jax.experimental.pallas import tpu as pltpu

_F32 = jnp.float32
_BF16 = jnp.bfloat16
_U32 = jnp.uint32

_EPS = 1e-6
_LRU_C = 8.0
_SWIGLU_LIMIT = 7.0
_SWIGLU_ALPHA = 1.702
_TOP_K = 4
_HEAD = 256

_V7X_VMEM_BYTES = 64 * 1024 * 1024
_VMEM_CAP = _V7X_VMEM_BYTES - 5 * 1024 * 1024
_SUBLANES_BF16 = 16
_SCAN_GROUP = 8

_TT_A_CAP = 512
_TT_B_CAP = 64
_EXPERT_CHUNK = 1280
_EXPERT_SUB = 256
_EXPERT_QUANTUM = 64


def _vmem_limit(estimate_bytes):
    return int(min(max(estimate_bytes + (6 << 20), 24 << 20), _VMEM_CAP))


def _largest_divisor(n, cap, mult):
    best = None
    for d in range(mult, min(n, cap) + 1, mult):
        if n % d == 0:
            best = d
    assert best is not None, (n, cap, mult)
    return best


def _round_up(n, m):
    return (n + m - 1) // m * m


def _sigmoid(x):
    return jax.nn.sigmoid(x)


def _gelu_tanh(x):
    c = math.sqrt(2.0 / math.pi)
    return 0.5 * x * (1.0 + jnp.tanh(c * (x + 0.044715 * (x * x * x))))


def _softplus(x):
    return jnp.maximum(x, 0.0) + jnp.log1p(jnp.exp(-jnp.abs(x)))


def _rms(x, g):
    ms = jnp.mean(x * x, axis=-1, keepdims=True)
    return (x * lax.rsqrt(ms + _EPS)) * g


def _rmsnorm_kernel(xm_ref, xt_ref, g_ref, x_ref, o_ref, *, n_main_blocks):
    i = pl.program_id(0)

    def emit(x):
        x_ref[...] = x
        o_ref[...] = _rms(x, g_ref[...]).astype(o_ref.dtype)

    @pl.when(i < n_main_blocks)
    def _():
        emit(xm_ref[...])

    @pl.when(i >= n_main_blocks)
    def _():
        emit(xt_ref[...])


def _rmsnorm_rows(x_main, x_tail, g, tr):
    n_main, d = x_main.shape
    rows = n_main + x_tail.shape[0]
    nmb = n_main // tr
    return pl.pallas_call(
        functools.partial(_rmsnorm_kernel, n_main_blocks=nmb),
        out_shape=(jax.ShapeDtypeStruct((rows, d), _F32),
                   jax.ShapeDtypeStruct((rows, d), _BF16)),
        grid=(rows // tr,),
        in_specs=[pl.BlockSpec((tr, d), lambda i: (jnp.minimum(i, nmb - 1), 0)),
                  pl.BlockSpec((tr, d), lambda i: (jnp.maximum(i - nmb, 0), 0)),
                  pl.BlockSpec((1, d), lambda i: (0, 0))],
        out_specs=(pl.BlockSpec((tr, d), lambda i: (i, 0)),
                   pl.BlockSpec((tr, d), lambda i: (i, 0))),
        compiler_params=pltpu.CompilerParams(
            dimension_semantics=("arbitrary",),
            vmem_limit_bytes=_vmem_limit(10 * tr * d * 4)),
        name="rmsnorm_rows",
    )(x_main, x_tail, g.reshape(1, d))


def _gmm_kernel(blk_ref, eid_ref, nrow_ref, *refs, n_lhs, n_w, has_bias, n_extra, pairs,
                tm, tn, sub, quantum, cast_rows, epilogue, packed, fused):
    del blk_ref, eid_ref
    pos = 0
    lhs_refs = refs[pos:pos + n_lhs]; pos += n_lhs
    w_refs = refs[pos:pos + n_w]; pos += n_w
    if has_bias:
        b_refs = refs[pos:pos + n_w]; pos += n_w
    ex_refs = refs[pos:pos + n_extra]; pos += n_extra
    o_ref = refs[pos]; pos += 1
    n_wbf = 1 if fused else n_w
    wbf_refs = refs[pos:pos + n_wbf]; pos += n_wbf
    if packed:
        lhs_refs = (refs[pos],)
        packed_ref = refs[0]

    nrows = nrow_ref[pl.program_id(0)]

    @pl.when(nrows > 0)
    def _():
        k_dim = w_refs[0].shape[0]

        def cast_body(c, carry):
            r = pl.multiple_of(c * cast_rows, cast_rows)
            for wi, w_ref in enumerate(w_refs):
                tile = w_ref[pl.ds(r, cast_rows), :].astype(_BF16)
                if fused:
                    wbf_refs[0][pl.ds(r, cast_rows), wi * tn:(wi + 1) * tn] = tile
                else:
                    wbf_refs[wi][pl.ds(r, cast_rows), :] = tile
            return carry

        lax.fori_loop(0, k_dim // cast_rows, cast_body, 0)

        if quantum is not None:
            per = sub // quantum
            n_q = jnp.maximum((nrows + quantum - 1) // quantum, 2 * per)

        if packed:
            half = k_dim // 2

            @pl.when(pl.program_id(1) == 0)
            def _():
                def unpack_body(q, carry):
                    r = pl.multiple_of(q * quantum, quantum)
                    w = packed_ref[pl.ds(r, quantum), :]
                    lo = lax.bitcast_convert_type(w << jnp.uint32(16), _F32)
                    hi = lax.bitcast_convert_type(w & jnp.uint32(0xFFFF0000), _F32)
                    lhs_refs[0][pl.ds(r, quantum), 0:half] = lo.astype(_BF16)
                    lhs_refs[0][pl.ds(r, quantum), half:k_dim] = hi.astype(_BF16)
                    return carry

                lax.fori_loop(0, n_q, unpack_body, 0)

        def compute(r, n):
            if fused:
                p = jnp.dot(lhs_refs[0][pl.ds(r, n), :], wbf_refs[0][...],
                            preferred_element_type=_F32)
                prods = [p[:, wi * tn:(wi + 1) * tn] for wi in range(n_w)]
            else:
                prods = [jnp.dot(lhs_refs[li][pl.ds(r, n), :], wbf_refs[wi][...],
                                 preferred_element_type=_F32) for (li, wi) in pairs]
            if has_bias:
                prods = [p + b_refs[wi][...] for p, (_, wi) in zip(prods, pairs)]
            extras = [e[pl.ds(r, n), :] for e in ex_refs]
            o_ref[pl.ds(r, n), :] = epilogue(prods, extras).astype(o_ref.dtype)

        def sub_body(s, carry):
            compute(pl.multiple_of(s * sub, sub), sub)
            return carry

        if quantum is None:
            lax.fori_loop(0, tm // sub, sub_body, 0)
        else:
            n_full = n_q // per
            n_tail = n_q - n_full * per
            n_plain = jnp.where(n_tail > 0, n_full - 2, n_full)

            def pair_body(s, carry):
                compute(pl.multiple_of(s * (2 * sub), 2 * sub), 2 * sub)
                return carry
            lax.fori_loop(0, n_plain // 2, pair_body, 0)

            @pl.when(n_plain % 2 == 1)
            def _():
                compute(pl.multiple_of((n_plain - 1) * sub, sub), sub)

            for tq in range(1, per):
                @pl.when(n_tail == tq)
                def _(tq=tq):
                    compute(pl.multiple_of((n_full - 2) * sub, sub), 2 * sub + tq * quantum)


def _gmm(lhs, ws, pairs, tables, *, tm, tn, sub, out_dtype, epilogue, name, quantum=None,
         biases=None, extras=(), packed=False, single_buffer_lhs=False):
    blk, eid, nrow = tables
    n_chunks = blk.shape[0]
    rows = lhs[0].shape[0]
    k_dim = ws[0].shape[1]
    n_out = ws[0].shape[2]
    nj = n_out // tn
    fused = len(lhs) == 1 and len(ws) > 1
    assert n_out % tn == 0 and tm % sub == 0 and rows % tm == 0
    assert quantum is None or (sub % quantum == 0 and tm >= 2 * sub)
    block_rows = sub if quantum is None else 3 * sub
    cast_rows = _largest_divisor(k_dim, 512, _SUBLANES_BF16)

    def jj(v, j, nrow_ref):
        return jnp.where(nrow_ref[v] > 0, j, nj - 1)

    lhs_mode = dict(pipeline_mode=pl.Buffered(1)) if single_buffer_lhs else {}
    in_specs = []
    for a in lhs:
        in_specs.append(pl.BlockSpec((tm, a.shape[1]), lambda v, j, b, e, n: (b[v], 0), **lhs_mode))
    for _ in ws:
        in_specs.append(pl.BlockSpec((None, k_dim, tn),
                                     lambda v, j, b, e, n: (e[v], 0, jj(v, j, n))))
    has_bias = biases is not None
    if has_bias:
        for _ in ws:
            in_specs.append(pl.BlockSpec((None, 1, tn),
                                         lambda v, j, b, e, n: (e[v], 0, jj(v, j, n))))
    for (_, off) in extras:
        in_specs.append(pl.BlockSpec(
            (tm, tn), lambda v, j, b, e, n, off=off: (b[v], off + jj(v, j, n))))
    out_spec = pl.BlockSpec((tm, tn), lambda v, j, b, e, n: (b[v], jj(v, j, n)))

    scratch = ([pltpu.VMEM((k_dim, len(ws) * tn), _BF16)] if fused
               else [pltpu.VMEM((k_dim, tn), _BF16) for _ in ws])
    if packed:
        scratch.append(pltpu.VMEM((tm, k_dim), _BF16))
    out_bytes = jnp.dtype(out_dtype).itemsize
    lhs_bufs = 1 if single_buffer_lhs else 2
    est = (lhs_bufs * len(lhs) * tm * k_dim * 2 + (tm * k_dim * 2 if packed else 0)
           + len(ws) * (2 * k_dim * tn * 4 + k_dim * tn * 2)
           + 2 * len(extras) * tm * tn * 4 + 2 * tm * tn * out_bytes
           + (3 + len(pairs)) * block_rows * tn * 4)
    kernel = functools.partial(
        _gmm_kernel, n_lhs=len(lhs), n_w=len(ws), has_bias=has_bias, n_extra=len(extras),
        pairs=tuple(pairs), tm=tm, tn=tn, sub=sub, quantum=quantum, cast_rows=cast_rows,
        epilogue=epilogue, packed=packed, fused=fused)
    args = list(lhs) + list(ws) + (list(biases) if has_bias else []) + [a for (a, _) in extras]
    return pl.pallas_call(
        kernel,
        out_shape=jax.ShapeDtypeStruct((rows, n_out), out_dtype),
        grid_spec=pltpu.PrefetchScalarGridSpec(
            num_scalar_prefetch=3,
            grid=(n_chunks, nj),
            in_specs=in_specs,
            out_specs=out_spec,
            scratch_shapes=scratch),
        compiler_params=pltpu.CompilerParams(
            dimension_semantics=("arbitrary", "arbitrary"),
            vmem_limit_bytes=_vmem_limit(est)),
        name=name,
    )(blk, eid, nrow, *args)


def _dense_tables(rows, tm):
    n = rows // tm
    return (jnp.arange(n, dtype=jnp.int32), jnp.zeros((n,), jnp.int32),
            jnp.full((n,), tm, jnp.int32))


def _lru_gates(ca, wr_ref, br_ref, wi_ref, bi_ref, lam_ref):
    cab = ca.astype(_BF16)
    n_heads = wr_ref.shape[0]
    zr, zi = [], []
    for hh in range(n_heads):
        c_h = cab[:, hh * _HEAD:(hh + 1) * _HEAD]
        zr.append(jnp.dot(c_h, wr_ref[hh].astype(_BF16), preferred_element_type=_F32))
        zi.append(jnp.dot(c_h, wi_ref[hh].astype(_BF16), preferred_element_type=_F32))
    zr = jnp.concatenate(zr, axis=-1) if n_heads > 1 else zr[0]
    zi = jnp.concatenate(zi, axis=-1) if n_heads > 1 else zi[0]
    r = _sigmoid(zr + br_ref[...])
    i = _sigmoid(zi + bi_ref[...])
    log_a = (-_LRU_C * r) * _softplus(-lam_ref[...])
    a = jnp.exp(log_a)
    u = jnp.sqrt(1.0 - a * a) * (i * ca)
    return a, u


def _branch_a_kernel(xm_ref, xa_ref, ga_ref, cw_ref, cb_ref, wr_ref, br_ref, wi_ref, bi_ref,
                     lam_ref, ya_ref, h_ref, hist_ref, buf, hcar, a_s, u_s, *, tt, kw, n_meta):
    t = pl.program_id(2)
    nt = pl.num_programs(2)
    hb = 8

    def run(x_tile, n):
        buf[hb:hb + n, :] = x_tile
        first = hb - (kw - 1)
        ca = cb_ref[...] + cw_ref[0:1, :] * buf[first:first + n, :]
        for k in range(1, kw):
            ca = ca + cw_ref[k:k + 1, :] * buf[first + k:first + k + n, :]
        tail = buf[hb + n - (kw - 1):hb + n, :]
        buf[first:hb, :] = tail
        a, u = _lru_gates(ca, wr_ref, br_ref, wi_ref, bi_ref, lam_ref)
        grouped = (n // _SCAN_GROUP, _SCAN_GROUP, a.shape[1])
        a = a.reshape(grouped)
        u = u.reshape(grouped)
        in_group = lax.broadcasted_iota(jnp.int32, grouped, 1)
        d = 1
        while d < _SCAN_GROUP:
            keep = in_group >= d
            a_sh = pltpu.roll(a, d, 1)
            u_sh = pltpu.roll(u, d, 1)
            u = jnp.where(keep, a * u_sh + u, u)
            a = jnp.where(keep, a * a_sh, a)
            d *= 2
        a_s[0:n, :] = a.reshape(n, grouped[2])
        u_s[0:n, :] = u.reshape(n, grouped[2])

        def group(g, carry):
            r = pl.multiple_of(g * _SCAN_GROUP, _SCAN_GROUP)
            h_g = u_s[pl.ds(r, _SCAN_GROUP), :] + a_s[pl.ds(r, _SCAN_GROUP), :] * carry
            u_s[pl.ds(r, _SCAN_GROUP), :] = h_g
            return h_g[_SCAN_GROUP - 1:_SCAN_GROUP, :]

        hcar[...] = lax.fori_loop(0, n // _SCAN_GROUP, group, hcar[...])
        return u_s[0:n, :], tail

    @pl.when(t == 0)
    def _():
        buf[0:hb, :] = jnp.zeros((hb, buf.shape[1]), _F32)
        hcar[...] = jnp.zeros_like(hcar)
        run(xm_ref[...], n_meta)

    h, tail = run(xa_ref[...], tt)
    ya_ref[...] = (h * _gelu_tanh(ga_ref[...])).astype(ya_ref.dtype)

    @pl.when(t == nt - 1)
    def _():
        h_ref[...] = h[tt - 1:tt, :]
        hist_ref[...] = tail


def _branch_a_prompt(proj, p, *, n_seq, t_len, d, rows_total, meta_row0, n_meta):
    kw = p["conv_a_w"].shape[0]
    tt = _largest_divisor(t_len, _TT_A_CAP, _SUBLANES_BF16)
    cw = _largest_divisor(d, 512, _HEAD)
    nt, nc = t_len // tt, d // cw
    hpc = cw // _HEAD
    ga_off = d // cw
    mblk = meta_row0 // n_meta
    vec = lambda b, c, t: (0, c)
    kernel = functools.partial(_branch_a_kernel, tt=tt, kw=kw, n_meta=n_meta)
    return pl.pallas_call(
        kernel,
        out_shape=(jax.ShapeDtypeStruct((rows_total, d), _BF16),
                   jax.ShapeDtypeStruct((n_seq, 1, d), _F32),
                   jax.ShapeDtypeStruct((n_seq, kw - 1, d), _F32)),
        grid=(n_seq, nc, nt),
        in_specs=[
            pl.BlockSpec((n_meta, cw), lambda b, c, t: (mblk, c)),
            pl.BlockSpec((tt, cw), lambda b, c, t: (b * nt + t, c)),
            pl.BlockSpec((tt, cw), lambda b, c, t: (b * nt + t, ga_off + c)),
            pl.BlockSpec((kw, cw), vec),
            pl.BlockSpec((1, cw), vec),
            pl.BlockSpec((hpc, _HEAD, _HEAD), lambda b, c, t: (c, 0, 0)),
            pl.BlockSpec((1, cw), vec),
            pl.BlockSpec((hpc, _HEAD, _HEAD), lambda b, c, t: (c, 0, 0)),
            pl.BlockSpec((1, cw), vec),
            pl.BlockSpec((1, cw), vec),
        ],
        out_specs=(
            pl.BlockSpec((tt, cw), lambda b, c, t: (b * nt + t, c)),
            pl.BlockSpec((None, 1, cw), lambda b, c, t: (b, 0, c)),
            pl.BlockSpec((None, kw - 1, cw), lambda b, c, t: (b, 0, c)),
        ),
        scratch_shapes=[pltpu.VMEM((8 + max(tt, n_meta), cw), _F32), pltpu.VMEM((1, cw), _F32),
                        pltpu.VMEM((max(tt, n_meta), cw), _F32),
                        pltpu.VMEM((max(tt, n_meta), cw), _F32)],
        compiler_params=pltpu.CompilerParams(
            dimension_semantics=("arbitrary", "arbitrary", "arbitrary"),
            vmem_limit_bytes=_vmem_limit(24 * tt * cw * 4)),
        name="branch_a_prompt",
    )(proj, proj, proj, p["conv_a_w"], p["conv_a_b"], p["w_r"], p["b_r"], p["w_i"], p["b_i"],
      p["lru_lambda"])


def _layernorm_silu(cb, g, b):
    mu = jnp.mean(cb, axis=-1, keepdims=True)
    xc = cb - mu
    y = xc * lax.rsqrt(jnp.mean(xc * xc, axis=-1, keepdims=True) + _EPS)
    y = y * g + b
    return y * _sigmoid(y)


def _branch_b_kernel(xgm_ref, gtm_ref, xg_ref, gt_ref, cw_ref, cb_ref, lg_ref, lb_ref,
                     yb_ref, hist_ref, buf, cbuf, shifted, *, tt, kw, lane_chunk, n_meta):
    t = pl.program_id(1)
    nt = pl.num_programs(1)
    hb = _round_up(kw - 1, 8)
    d = buf.shape[1]

    @pl.when(t == 0)
    def _():
        buf[0:hb, :] = jnp.zeros((hb, d), _F32)
        buf[hb - n_meta:hb, :] = xgm_ref[...] * _sigmoid(gtm_ref[...])

    buf[hb:hb + tt, :] = xg_ref[...] * _sigmoid(gt_ref[...])
    first = hb - (kw - 1)
    for c0 in range(0, d, lane_chunk):
        cs = slice(c0, c0 + lane_chunk)
        acc = jnp.broadcast_to(cb_ref[:, cs], (tt, lane_chunk))
        for res in range(_SCAN_GROUP):
            taps = [k for k in range(kw) if (first + k) % _SCAN_GROUP == res]
            if not taps:
                continue
            span = _SCAN_GROUP * max((first + k) // _SCAN_GROUP for k in taps) + tt
            shifted[0:span, :] = buf[res:res + span, cs]
            for k in taps:
                q0 = (first + k) // _SCAN_GROUP * _SCAN_GROUP
                acc = acc + cw_ref[k:k + 1, cs] * shifted[q0:q0 + tt, :]
        cbuf[:, cs] = acc
    tail = buf[hb + tt - (kw - 1):hb + tt, :]
    buf[first:hb, :] = tail
    yb_ref[...] = _layernorm_silu(cbuf[...], lg_ref[...], lb_ref[...]).astype(yb_ref.dtype)

    @pl.when(t == nt - 1)
    def _():
        hist_ref[...] = tail


def _branch_b_prompt(proj, p, *, n_seq, t_len, d, rows_total, meta_row0, n_meta):
    kw = p["conv_b_w"].shape[0]
    tt = _largest_divisor(t_len, _TT_B_CAP, _SUBLANES_BF16)
    nt = t_len // tt
    hb = _round_up(kw - 1, 8)
    assert n_meta <= hb and tt >= kw - 1
    mblk = meta_row0 // n_meta
    vec = lambda b, t: (0, 0)
    kernel = functools.partial(_branch_b_kernel, tt=tt, kw=kw, n_meta=n_meta,
                               lane_chunk=_largest_divisor(d, 256, 128))
    return pl.pallas_call(
        kernel,
        out_shape=(jax.ShapeDtypeStruct((rows_total, d), _BF16),
                   jax.ShapeDtypeStruct((n_seq, kw - 1, d), _F32)),
        grid=(n_seq, nt),
        in_specs=[
            pl.BlockSpec((n_meta, d), lambda b, t: (mblk, 2)),
            pl.BlockSpec((n_meta, d), lambda b, t: (mblk, 3)),
            pl.BlockSpec((tt, d), lambda b, t: (b * nt + t, 2)),
            pl.BlockSpec((tt, d), lambda b, t: (b * nt + t, 3)),
            pl.BlockSpec((kw, d), vec),
            pl.BlockSpec((1, d), vec),
            pl.BlockSpec((1, d), vec),
            pl.BlockSpec((1, d), vec),
        ],
        out_specs=(
            pl.BlockSpec((tt, d), lambda b, t: (b * nt + t, 0)),
            pl.BlockSpec((None, kw - 1, d), lambda b, t: (b, 0, 0)),
        ),
        scratch_shapes=[pltpu.VMEM((hb + tt, d), _F32), pltpu.VMEM((tt, d), _F32),
                        pltpu.VMEM((hb + tt, _largest_divisor(d, 256, 128)), _F32)],
        compiler_params=pltpu.CompilerParams(
            dimension_semantics=("arbitrary", "arbitrary"),
            vmem_limit_bytes=_vmem_limit(16 * (hb + tt) * d * 4)),
        name="branch_b_prompt",
    )(proj, proj, proj, proj, p["conv_b_w"], p["conv_b_b"], p["ln_b_g"], p["ln_b_b"])


def _sample_kernel(xa_ref, ga_ref, xg_ref, gt_ref, sh_ref, sa_ref, sb_ref,
                   caw_ref, cab_ref, wr_ref, br_ref, wi_ref, bi_ref, lam_ref,
                   cbw_ref, cbb_ref, lg_ref, lb_ref, ya_in, yb_in,
                   ya_ref, yb_ref, oh_ref, oa_ref, ob_ref, *, n_real, kwa, kwb):
    del ya_in, yb_in
    i = pl.program_id(0)

    def conv_step(hist_ref, new_ref, x_new, w_ref, b_ref, kw):
        acc = b_ref[...] + w_ref[kw - 1:kw, :] * x_new
        for k in range(kw - 1):
            h_k = hist_ref[k]
            acc = acc + w_ref[k:k + 1, :] * h_k
            if k >= 1:
                new_ref[k - 1] = h_k
        new_ref[kw - 2] = x_new
        return acc

    @pl.when(i < n_real)
    def _():
        glu = xg_ref[...] * _sigmoid(gt_ref[...])
        cb = conv_step(sb_ref, ob_ref, glu, cbw_ref, cbb_ref, kwb)
        ca = conv_step(sa_ref, oa_ref, xa_ref[...], caw_ref, cab_ref, kwa)
        a, u = _lru_gates(ca, wr_ref, br_ref, wi_ref, bi_ref, lam_ref)
        h = a * sh_ref[...] + u
        oh_ref[...] = h
        ya_ref[...] = (h * _gelu_tanh(ga_ref[...])).astype(ya_ref.dtype)
        yb_ref[...] = _layernorm_silu(cb, lg_ref[...], lb_ref[...]).astype(yb_ref.dtype)

    @pl.when(i >= n_real)
    def _():
        ya_ref[...] = jnp.zeros_like(ya_ref)
        yb_ref[...] = jnp.zeros_like(yb_ref)


def _branches_sample(proj, ya, yb, state_h, state_a, state_b, p, *, row0, n_s, d, rows_total):
    tr = _SUBLANES_BF16
    kwa, kwb = p["conv_a_w"].shape[0], p["conv_b_w"].shape[0]
    n_real = n_s // tr
    n_blocks = (rows_total - row0) // tr
    b0 = row0 // tr
    n_heads = d // _HEAD
    full = lambda i: (0, 0)
    st = lambda i: (jnp.minimum(i, n_real - 1), 0)
    st3 = lambda i: (0, jnp.minimum(i, n_real - 1), 0)
    kernel = functools.partial(_sample_kernel, n_real=n_real, kwa=kwa, kwb=kwb)
    col = lambda c: pl.BlockSpec((tr, d), lambda i, c=c: (b0 + i, c))
    est = (4 * tr * (kwa + kwb) * d * 4 + 2 * n_heads * _HEAD * _HEAD * 4
           + 24 * tr * d * 4 + 2 * (kwb + 16) * d * 4)
    return pl.pallas_call(
        kernel,
        out_shape=(jax.ShapeDtypeStruct((rows_total, d), _BF16),
                   jax.ShapeDtypeStruct((rows_total, d), _BF16),
                   jax.ShapeDtypeStruct((n_s, d), _F32),
                   jax.ShapeDtypeStruct((kwa - 1, n_s, d), _F32),
                   jax.ShapeDtypeStruct((kwb - 1, n_s, d), _F32)),
        grid=(n_blocks,),
        in_specs=[
            col(0), col(1), col(2), col(3),
            pl.BlockSpec((tr, d), st),
            pl.BlockSpec((kwa - 1, tr, d), st3),
            pl.BlockSpec((kwb - 1, tr, d), st3),
            pl.BlockSpec((kwa, d), full), pl.BlockSpec((1, d), full),
            pl.BlockSpec((n_heads, _HEAD, _HEAD), lambda i: (0, 0, 0),
                         pipeline_mode=pl.Buffered(1)), pl.BlockSpec((1, d), full),
            pl.BlockSpec((n_heads, _HEAD, _HEAD), lambda i: (0, 0, 0),
                         pipeline_mode=pl.Buffered(1)), pl.BlockSpec((1, d), full),
            pl.BlockSpec((1, d), full),
            pl.BlockSpec((kwb, d), full), pl.BlockSpec((1, d), full),
            pl.BlockSpec((1, d), full), pl.BlockSpec((1, d), full),
            pl.BlockSpec(memory_space=pl.ANY), pl.BlockSpec(memory_space=pl.ANY),
        ],
        out_specs=(
            pl.BlockSpec((tr, d), lambda i: (b0 + i, 0)),
            pl.BlockSpec((tr, d), lambda i: (b0 + i, 0)),
            pl.BlockSpec((tr, d), st),
            pl.BlockSpec((kwa - 1, tr, d), st3),
            pl.BlockSpec((kwb - 1, tr, d), st3),
        ),
        input_output_aliases={18: 0, 19: 1},
        compiler_params=pltpu.CompilerParams(
            dimension_semantics=("arbitrary",),
            vmem_limit_bytes=_vmem_limit(est)),
        name="branches_sample",
    )(proj, proj, proj, proj, state_h, state_a, state_b,
      p["conv_a_w"], p["conv_a_b"], p["w_r"], p["b_r"], p["w_i"], p["b_i"], p["lru_lambda"],
      p["conv_b_w"], p["conv_b_b"], p["ln_b_g"], p["ln_b_b"], ya, yb)


def _router_kernel(x_ref, g_ref, wr_ref, br_ref, xp_ref, te_ref, tg_ref, rk_ref, cnt_ref,
                   carry, *, n_exp):
    i = pl.program_id(0)

    @pl.when(i == 0)
    def _():
        carry[...] = jnp.zeros_like(carry)

    xn = _rms(x_ref[...], g_ref[...])
    xh = xn.astype(_BF16)
    xhf = xh.astype(_F32)
    half = xn.shape[1] // 2
    bits = lax.bitcast_convert_type(xhf, _U32)
    xp_ref[...] = (bits[:, 0:half] >> jnp.uint32(16)) | bits[:, half:2 * half]

    w = wr_ref[...]
    xl = (xn - xhf).astype(_BF16)
    wh = w.astype(_BF16)
    wl = (w - wh.astype(_F32)).astype(_BF16)
    logits = (jnp.dot(xh, wh, preferred_element_type=_F32)
              + (jnp.dot(xl, wh, preferred_element_type=_F32)
                 + jnp.dot(xh, wl, preferred_element_type=_F32))) + br_ref[...]

    tb = xn.shape[0]
    lane = lax.broadcasted_iota(jnp.int32, (tb, n_exp), 1)
    work = logits
    vals, idxs, hots = [], [], []
    for _ in range(_TOP_K):
        m = jnp.max(work, axis=-1, keepdims=True)
        idx = jnp.min(jnp.where(work == m, lane, n_exp), axis=-1, keepdims=True)
        hot = lane == idx
        vals.append(m); idxs.append(idx); hots.append(hot)
        work = jnp.where(hot, -jnp.inf, work)
    es = [jnp.exp(v - vals[0]) for v in vals]
    den = es[0] + es[1] + es[2] + es[3]
    tg_ref[...] = jnp.concatenate([e / den for e in es], axis=-1)
    te_ref[...] = jnp.concatenate(idxs, axis=-1)

    sel = (hots[0] | hots[1] | hots[2] | hots[3]).astype(_F32)
    r_i = lax.broadcasted_iota(jnp.int32, (tb, tb), 0)
    c_i = lax.broadcasted_iota(jnp.int32, (tb, tb), 1)
    tri = (c_i < r_i).astype(_BF16)
    before = jnp.dot(tri, sel.astype(_BF16), preferred_element_type=_F32) + carry[...]
    ranks = [jnp.sum(jnp.where(h, before, 0.0), axis=-1, keepdims=True) for h in hots]
    rk_ref[...] = jnp.concatenate(ranks, axis=-1).astype(jnp.int32)
    total = carry[...] + jnp.sum(sel, axis=0, keepdims=True)
    carry[...] = total
    cnt_ref[...] = total.astype(jnp.int32)


def _router(x2, g, w_router, b_router):
    rows, d = x2.shape
    n_exp = w_router.shape[1]
    tb = _largest_divisor(rows, 256, 8)
    tok = lambda i: (i, 0)
    full = lambda i: (0, 0)
    return pl.pallas_call(
        functools.partial(_router_kernel, n_exp=n_exp),
        out_shape=(jax.ShapeDtypeStruct((rows, d // 2), _U32),
                   jax.ShapeDtypeStruct((rows, _TOP_K), jnp.int32),
                   jax.ShapeDtypeStruct((rows, _TOP_K), _F32),
                   jax.ShapeDtypeStruct((rows, _TOP_K), jnp.int32),
                   jax.ShapeDtypeStruct((1, n_exp), jnp.int32)),
        grid=(rows // tb,),
        in_specs=[pl.BlockSpec((tb, d), tok), pl.BlockSpec((1, d), full),
                  pl.BlockSpec((d, n_exp), full), pl.BlockSpec((1, n_exp), full)],
        out_specs=(pl.BlockSpec((tb, d // 2), tok), pl.BlockSpec((tb, _TOP_K), tok),
                   pl.BlockSpec((tb, _TOP_K), tok), pl.BlockSpec((tb, _TOP_K), tok),
                   pl.BlockSpec((1, n_exp), full)),
        scratch_shapes=[pltpu.VMEM((1, n_exp), _F32)],
        compiler_params=pltpu.CompilerParams(
            dimension_semantics=("arbitrary",),
            vmem_limit_bytes=_vmem_limit(8 * tb * d * 4)),
        name="router",
    )(x2, g.reshape(1, d), w_router, b_router.reshape(1, n_exp))


def _dispatch_kernel(slot_ref, padrow_ref, padn_ref, x_ref, o_hbm, stage, zrow, sem,
                     *, blk, n_blocks, n_chunks):
    i = pl.program_id(0)

    def row_copy(p, src_row, dst_row):
        return pltpu.make_async_copy(stage.at[p, pl.ds(src_row, 1), :],
                                     o_hbm.at[pl.ds(dst_row, 1), :], sem.at[p])

    def drain(p, n):
        def body(t, carry):
            row_copy(p, 0, 0).wait()
            return carry
        lax.fori_loop(0, n, body, 0)

    def block_wait(p):
        for _ in range(_TOP_K):
            pltpu.make_async_copy(stage.at[p], o_hbm.at[pl.ds(0, blk), :], sem.at[p]).wait()

    for p in (0, 1):
        @pl.when(i % 2 == p)
        def _(p=p):
            @pl.when(i >= 2)
            def _():
                block_wait(p)

            stage[p] = x_ref[...]

            def issue(t, carry):
                tok = i * blk + t
                for k in range(_TOP_K):
                    row_copy(p, t, slot_ref[tok * _TOP_K + k]).start()
                return carry
            lax.fori_loop(0, blk, issue, 0)

    @pl.when(i == n_blocks - 1)
    def _():
        for p in range(min(n_blocks, 2)):
            block_wait(p)
        zrow[...] = jnp.zeros_like(zrow)

        def zero_fill(v, carry):
            n = padn_ref[v]

            def body(r, c2):
                pltpu.make_async_copy(zrow.at[pl.ds(0, 1), :],
                                      o_hbm.at[pl.ds(padrow_ref[v] + r, 1), :], sem.at[0]).start()
                return c2
            lax.fori_loop(0, n, body, 0)
            drain(0, n)
            return carry
        lax.fori_loop(0, n_chunks, zero_fill, 0)


def _dispatch(xp, slots, pad_row, pad_n, *, n_rows_out):
    n_tok, half = xp.shape
    blk = _largest_divisor(n_tok, 256, 8)
    n_blocks = n_tok // blk
    return pl.pallas_call(
        functools.partial(_dispatch_kernel, blk=blk, n_blocks=n_blocks, n_chunks=pad_n.shape[0]),
        out_shape=jax.ShapeDtypeStruct((n_rows_out, half), _U32),
        grid_spec=pltpu.PrefetchScalarGridSpec(
            num_scalar_prefetch=3,
            grid=(n_blocks,),
            in_specs=[pl.BlockSpec((blk, half), lambda i, s, pr, pn: (i, 0))],
            out_specs=pl.BlockSpec(memory_space=pl.ANY),
            scratch_shapes=[pltpu.VMEM((2, blk, half), _U32), pltpu.VMEM((8, half), _U32),
                            pltpu.SemaphoreType.DMA((2,))]),
        compiler_params=pltpu.CompilerParams(
            dimension_semantics=("arbitrary",),
            vmem_limit_bytes=_vmem_limit(6 * blk * half * 4)),
        name="dispatch",
    )(slots, pad_row, pad_n, xp)


def _combine_kernel(slot_ref, ys_hbm, g_ref, x_ref, gf_ref, om_ref, ot_ref, buf, sem,
                    *, tb, n_main_blocks):
    i = pl.program_id(0)
    n = pl.num_programs(0)

    def issue(b, p):
        base = b * (tb * _TOP_K)

        def body(r, carry):
            for k in range(_TOP_K):
                pltpu.make_async_copy(ys_hbm.at[pl.ds(slot_ref[base + r * _TOP_K + k], 1), :],
                                      buf.at[p, k, pl.ds(r, 1), :], sem.at[p]).start()
            return carry
        lax.fori_loop(0, tb, body, 0)

    @pl.when(i == 0)
    def _():
        issue(i, 0)

    for p in (0, 1):
        @pl.when(jnp.logical_and(i + 1 < n, (i + 1) % 2 == p))
        def _(p=p):
            issue(i + 1, p)

    for p in (0, 1):
        @pl.when(i % 2 == p)
        def _(p=p):
            for k in range(_TOP_K):
                pltpu.make_async_copy(ys_hbm.at[pl.ds(0, tb), :], buf.at[p, k], sem.at[p]).wait()
            g = g_ref[...]
            moe = g[:, 0:1] * buf[p, 0]
            for k in range(1, _TOP_K):
                moe = moe + g[:, k:k + 1] * buf[p, k]
            y = _rms(x_ref[...] + moe, gf_ref[...])

            @pl.when(i < n_main_blocks)
            def _():
                om_ref[...] = y

            @pl.when(i >= n_main_blocks)
            def _():
                ot_ref[...] = y


def _combine(ys, slots, gates, x2, g_final, *, n_main, tb):
    rows, d = x2.shape
    nmb = n_main // tb
    tok = lambda i, s: (i, 0)
    return pl.pallas_call(
        functools.partial(_combine_kernel, tb=tb, n_main_blocks=nmb),
        out_shape=(jax.ShapeDtypeStruct((n_main, d), _F32),
                   jax.ShapeDtypeStruct((rows - n_main, d), _F32)),
        grid_spec=pltpu.PrefetchScalarGridSpec(
            num_scalar_prefetch=1,
            grid=(rows // tb,),
            in_specs=[pl.BlockSpec(memory_space=pl.ANY),
                      pl.BlockSpec((tb, _TOP_K), tok),
                      pl.BlockSpec((tb, d), tok),
                      pl.BlockSpec((1, d), lambda i, s: (0, 0))],
            out_specs=(pl.BlockSpec((tb, d), lambda i, s: (jnp.minimum(i, nmb - 1), 0)),
                       pl.BlockSpec((tb, d), lambda i, s: (jnp.maximum(i - nmb, 0), 0))),
            scratch_shapes=[pltpu.VMEM((2, _TOP_K, tb, d), _F32),
                            pltpu.SemaphoreType.DMA((2,))]),
        compiler_params=pltpu.CompilerParams(
            dimension_semantics=("arbitrary",),
            vmem_limit_bytes=_vmem_limit((2 * _TOP_K + 10) * tb * d * 4)),
        name="combine",
    )(slots, ys, gates, x2, g_final.reshape(1, d))


def _moe_tables(counts, top_e, rank, *, tm, sub, quantum, n_chunks):
    n_exp = counts.shape[0]
    chunks_e = (counts + tm - 1) // tm
    chunk_end = jnp.cumsum(chunks_e)
    chunk_start = chunk_end - chunks_e
    n_used = chunk_end[-1]
    v = jnp.arange(n_chunks, dtype=jnp.int32)
    v_eff = jnp.minimum(v, n_used - 1)
    e_v = jnp.minimum(jnp.searchsorted(chunk_end, v_eff, side="right"), n_exp - 1).astype(jnp.int32)
    within = v_eff - chunk_start[e_v]
    rows_v = jnp.where(v < n_used, jnp.clip(counts[e_v] - within * tm, 0, tm), 0).astype(jnp.int32)
    pad_row = (v_eff * tm + rows_v).astype(jnp.int32)
    filled = jnp.maximum((rows_v + quantum - 1) // quantum * quantum, 2 * sub)
    pad_n = jnp.where(rows_v > 0, filled - rows_v, 0).astype(jnp.int32)
    slot = (chunk_start[top_e] * tm + rank).astype(jnp.int32)
    return (v_eff.astype(jnp.int32), e_v, rows_v), pad_row, pad_n, slot


def kernel(x_prompt, x_sample, state_lru_h, state_lru_conv, state_conf_conv, meta_tokens,
           norm_mix_g, w_in, conv_a_w, conv_a_b, w_r, b_r, w_i, b_i, lru_lambda,
           conv_b_w, conv_b_b, ln_b_g, ln_b_b, w_a_out, w_b_out, w_o, norm_ffn_g,
           w_router, b_router, w_gate, b_gate, w_up, b_up, w_down, b_down, norm_final_g):
    assert w_in.shape[0] == 1, "one layer"
    n_b, seq, d = x_prompt.shape
    n_s = x_sample.shape[0]
    n_meta = meta_tokens.shape[0]
    n_main = n_b * seq
    n_exp = w_router.shape[2]
    assert x_sample.shape[1] == 1 and n_s % _SUBLANES_BF16 == 0 and seq % _SUBLANES_BF16 == 0
    assert d % _HEAD == 0 and w_r.shape[2] == _HEAD
    assert n_meta % _SUBLANES_BF16 == 0 and (n_main + n_s) % n_meta == 0

    tr = _largest_divisor(n_main, 256, _SUBLANES_BF16)
    rows_tail = _round_up(n_s + n_meta, tr)
    rows = n_main + rows_tail
    meta_row0 = n_main + n_s
    tm_d = _largest_divisor(rows, 2112, 32)
    tm_o = _largest_divisor(rows, 1056, 32)
    sub_d = _largest_divisor(tm_d, 704, _SUBLANES_BF16)
    sub_o = _largest_divisor(tm_o, 704, _SUBLANES_BF16)
    tm_2 = _largest_divisor(rows, 1408, 32)
    sub_2 = _largest_divisor(tm_2, 704, _SUBLANES_BF16)
    tn = _largest_divisor(d, 512, 128)
    tn_2 = _largest_divisor(d, 256, 128)
    big = rows >= 4096
    tm_e = _EXPERT_CHUNK if big else 128
    sub_e = _EXPERT_SUB if big else 64
    q_e = _EXPERT_QUANTUM if big else 16
    n_pairs = rows * _TOP_K
    n_chunks = n_exp + n_pairs // tm_e

    x_tail = jnp.concatenate([x_sample.reshape(n_s, d), meta_tokens.astype(x_prompt.dtype),
                              jnp.zeros((rows_tail - n_s - n_meta, d), x_prompt.dtype)], axis=0)

    p = dict(conv_a_w=conv_a_w[0], conv_a_b=conv_a_b[0].reshape(1, d),
             w_r=w_r[0], b_r=b_r[0].reshape(1, d), w_i=w_i[0], b_i=b_i[0].reshape(1, d),
             lru_lambda=lru_lambda[0].reshape(1, d),
             conv_b_w=conv_b_w[0], conv_b_b=conv_b_b[0].reshape(1, d),
             ln_b_g=ln_b_g[0].reshape(1, d), ln_b_b=ln_b_b[0].reshape(1, d))

    x, xn = _rmsnorm_rows(x_prompt.reshape(n_main, d), x_tail, norm_mix_g[0], tr)
    proj = _gmm([xn], [w_in], [(0, 0)], _dense_tables(rows, tm_d),
                tm=tm_d, tn=tn, sub=sub_d, out_dtype=_F32, single_buffer_lhs=True,
                epilogue=lambda pr, ex: pr[0], name="in_proj")
    seq_args = dict(n_seq=n_b, t_len=seq, d=d, rows_total=rows, meta_row0=meta_row0, n_meta=n_meta)
    ya, h_p, ha_p = _branch_a_prompt(proj, p, **seq_args)
    yb, hb_p = _branch_b_prompt(proj, p, **seq_args)
    ya, yb, h_s, ha_s, hb_s = _branches_sample(
        proj, ya, yb, state_lru_h[0], jnp.swapaxes(state_lru_conv[0], 0, 1),
        jnp.swapaxes(state_conf_conv[0], 0, 1), p,
        row0=n_main, n_s=n_s, d=d, rows_total=rows)
    gate_off = 4 * d // tn_2
    merged = _gmm([ya, yb], [w_a_out, w_b_out], [(0, 0), (1, 1)],
                  _dense_tables(rows, tm_2), tm=tm_2, tn=tn_2, sub=sub_2,
                  out_dtype=_BF16, name="merge_proj", single_buffer_lhs=True,
                  extras=[(proj, gate_off), (proj, gate_off + d // tn_2)],
                  epilogue=lambda pr, ex: _sigmoid(ex[0]) * pr[0] + _sigmoid(ex[1]) * pr[1])
    x2 = _gmm([merged], [w_o], [(0, 0)], _dense_tables(rows, tm_o),
              tm=tm_o, tn=tn, sub=sub_o, out_dtype=_F32, name="out_proj",
              extras=[(x, 0)], epilogue=lambda pr, ex: ex[0] + pr[0])

    xp, top_e, gates, rank, counts = _router(x2, norm_ffn_g[0], w_router[0], b_router[0])
    tables, pad_row, pad_n, slot = _moe_tables(
        counts[0], top_e, rank, tm=tm_e, sub=sub_e, quantum=q_e, n_chunks=n_chunks)
    slots = slot.reshape(-1)
    xs = _dispatch(xp, slots, pad_row, pad_n, n_rows_out=n_chunks * tm_e)

    def swiglu(pr, ex):
        g = jnp.minimum(pr[0], _SWIGLU_LIMIT)
        u = jnp.clip(pr[1], -_SWIGLU_LIMIT, _SWIGLU_LIMIT)
        return g * _sigmoid(_SWIGLU_ALPHA * g) * (u + 1.0)

    hid = _gmm([xs], [w_gate[0], w_up[0]], [(0, 0), (0, 1)], tables, tm=tm_e, tn=tn_2,
               sub=sub_e, quantum=q_e, out_dtype=_BF16, name="expert_up", epilogue=swiglu,
               packed=True,
               biases=[b_gate[0][:, None, :], b_up[0][:, None, :]])
    ys = _gmm([hid], [w_down[0]], [(0, 0)], tables, tm=tm_e, tn=tn, sub=sub_e, quantum=q_e,
              out_dtype=_F32, name="expert_down", epilogue=lambda pr, ex: pr[0],
              biases=[b_down[0][:, None, :]])
    tb_c = _largest_divisor(math.gcd(n_main, rows_tail), 64, 8)
    y_main, y_tail = _combine(ys, slots, gates, x2, norm_final_g, n_main=n_main, tb=tb_c)

    return (y_main.reshape(n_b, seq, d), y_tail[:n_s].reshape(n_s, 1, d),
            h_p.reshape(1, n_b, d), ha_p[None], hb_p[None], h_s[None],
            jnp.swapaxes(ha_s, 0, 1)[None], jnp.swapaxes(hb_s, 0, 1)[None])
```
